```python
import jax, jax.numpy as jnp
from jax import lax
import numpy as np

D_MODEL = 4096
BATCH = 2
SEQ = 8192
DEPTH = 4

CHUNK = 64
D_FF = 4096
FFN_RES_WEIGHT = 0.5
NORM_EPS = 1e-6
D_RG = D_MODEL // 2
RG_BLOCK = 256
RG_HEADS = D_RG // RG_BLOCK
CONV_WIDTH = 4
RG_C = 8.0
A_MIN = 0.9
A_MAX = 0.999
D_HG = D_MODEL // 2
HG_HEAD_DIM = 128
HG_HEADS = D_HG // HG_HEAD_DIM
IN_COLS = 2 * D_RG + 4 * D_HG + 2 * D_MODEL
SPLIT_POINTS = (D_RG, 2 * D_RG, 2 * D_RG + D_HG, 2 * D_RG + 2 * D_HG, 2 * D_RG + 3 * D_HG, 2 * D_RG + 4 * D_HG, 2 * D_RG + 4 * D_HG + D_MODEL)

kernel_name = "hybrid_rglru_hgrn2_macaron"


def rms_norm(x, gain):
    xf = x.astype(jnp.float32)
    y = xf * lax.rsqrt(jnp.mean(xf * xf, axis=-1, keepdims=True) + NORM_EPS) * gain.astype(jnp.float32)
    return y.astype(x.dtype)


def swiglu(x, w_gate, w_up, w_down):
    return (jax.nn.silu(x @ w_gate) * (x @ w_up)) @ w_down


def causal_depthwise_conv(x, w, b):
    S = x.shape[1]
    xp = jnp.pad(x, ((0, 0), (CONV_WIDTH - 1, 0), (0, 0)))
    y = b
    for k in range(CONV_WIDTH):
        y = y + xp[:, k:k + S, :] * w[k]
    return y


def rglru_branch(xa, ya, conv_w, conv_b, wa, ba, wx, bx, lam):
    B, S, _ = xa.shape
    xc = causal_depthwise_conv(xa.astype(jnp.float32), conv_w.astype(jnp.float32), conv_b.astype(jnp.float32))
    xh = xc.reshape(B, S, RG_HEADS, RG_BLOCK)
    r = jax.nn.sigmoid(jnp.einsum('bshi,hij->bshj', xh, wa.astype(jnp.float32)) + ba.astype(jnp.float32)).reshape(B, S, D_RG)
    i = jax.nn.sigmoid(jnp.einsum('bshi,hij->bshj', xh, wx.astype(jnp.float32)) + bx.astype(jnp.float32)).reshape(B, S, D_RG)
    log_a = RG_C * r * jax.nn.log_sigmoid(lam.astype(jnp.float32))
    a = jnp.exp(log_a)
    b_in = jnp.sqrt(-jnp.expm1(2.0 * log_a)) * (i * xc)

    def combine(left, right):
        a_l, b_l = left
        a_r, b_r = right
        return a_l * a_r, a_r * b_l + b_r

    _, h = lax.associative_scan(combine, (a, b_in), axis=1)
    return (h * jax.nn.gelu(ya.astype(jnp.float32))).astype(xa.dtype)


def hgrn2_lower_bounds(lb_param):
    p = jax.nn.softmax(lb_param.astype(jnp.float32), axis=0)
    c = jnp.cumsum(p, axis=0)
    return c - c[0:1]


def chunkwise_gated_recurrence(q, log_f, k, v):
    B, S, H, dk = q.shape
    dv = v.shape[-1]
    n = S // CHUNK

    def to_chunks(t):
        return t.reshape(B, n, CHUNK, H, t.shape[-1]).transpose(1, 0, 3, 2, 4)

    causal = jnp.tril(jnp.ones((CHUNK, CHUNK), dtype=bool))[None, None, :, :, None]

    def step(state, inp):
        qc, gc, kc, vc = inp
        bcum = jnp.cumsum(gc, axis=2)
        o_inter = jnp.einsum('bhtk,bhkv->bhtv', qc * jnp.exp(bcum), state)
        rel = jnp.where(causal, bcum[:, :, :, None, :] - bcum[:, :, None, :, :], -jnp.inf)
        scores = jnp.einsum('bhtk,bhsk,bhtsk->bhts', qc, kc, jnp.exp(rel))
        o = o_inter + jnp.einsum('bhts,bhsv->bhtv', scores, vc)
        b_last = bcum[:, :, -1:, :]
        k_dec = kc * jnp.exp(b_last - bcum)
        new_state = jnp.exp(b_last[:, :, 0, :])[..., None] * state + jnp.einsum('bhsk,bhsv->bhkv', k_dec, vc)
        return new_state, o

    init = jnp.zeros((B, H, dk, dv), jnp.float32)
    _, o = lax.scan(step, init, (to_chunks(q), to_chunks(log_f), to_chunks(k), to_chunks(v)))
    return o.transpose(1, 0, 3, 2, 4).reshape(B, S, H, dv)


def hgrn2_branch(q_pre, f_pre, i_pre, g_pre, lb, out_gain):
    B, S, _ = q_pre.shape
    shp = (B, S, HG_HEADS, HG_HEAD_DIM)
    lbh = lb.reshape(HG_HEADS, HG_HEAD_DIM)
    zf = f_pre.astype(jnp.float32).reshape(shp)
    q = jax.nn.silu(q_pre.astype(jnp.float32)).reshape(shp)
    log_f = jnp.logaddexp(jnp.log(lbh), jnp.log1p(-lbh) + jax.nn.log_sigmoid(zf))
    k = (1.0 - lbh) * jax.nn.sigmoid(-zf)
    v = i_pre.astype(jnp.float32).reshape(shp)
    o = chunkwise_gated_recurrence(q, log_f, k, v)
    o = o * lax.rsqrt(jnp.mean(o * o, axis=-1, keepdims=True) + NORM_EPS) * out_gain.astype(jnp.float32).reshape(HG_HEADS, HG_HEAD_DIM)
    o = o * jax.nn.silu(g_pre.astype(jnp.float32).reshape(shp))
    return o.reshape(B, S, D_HG).astype(q_pre.dtype)


def setup_inputs(seed: int = 0) -> dict:
    key = jax.random.key(seed)
    ks = jax.random.split(key, 32)
    f32 = jnp.float32

    def nrm(k, shape, scale):
        return jax.random.normal(k, shape, f32) * scale

    def gain(k, shape):
        return 1.0 + 0.02 * jax.random.normal(k, shape, f32)

    u = jax.random.uniform(ks[14], (DEPTH, D_RG), f32, A_MIN, A_MAX)
    s = u ** (1.0 / RG_C)
    rg_lambda = jnp.log(s) - jnp.log1p(-s)
    return {
        "x": nrm(ks[0], (BATCH, SEQ, D_MODEL), 1.0),
        "ffn1_norm": gain(ks[1], (DEPTH, D_MODEL)),
        "ffn1_w_gate": nrm(ks[2], (DEPTH, D_MODEL, D_FF), D_MODEL ** -0.5),
        "ffn1_w_up": nrm(ks[3], (DEPTH, D_MODEL, D_FF), D_MODEL ** -0.5),
        "ffn1_w_down": nrm(ks[4], (DEPTH, D_FF, D_MODEL), D_FF ** -0.5),
        "mix_norm": gain(ks[5], (DEPTH, D_MODEL)),
        "w_in": nrm(ks[6], (DEPTH, D_MODEL, IN_COLS), D_MODEL ** -0.5),
        "conv_w": nrm(ks[7], (DEPTH, CONV_WIDTH, D_RG), CONV_WIDTH ** -0.5),
        "conv_b": nrm(ks[8], (DEPTH, D_RG), 0.01),
        "rg_wa": nrm(ks[9], (DEPTH, RG_HEADS, RG_BLOCK, RG_BLOCK), RG_BLOCK ** -0.5),
        "rg_ba": nrm(ks[10], (DEPTH, RG_HEADS, RG_BLOCK), 0.01),
        "rg_wx": nrm(ks[11], (DEPTH, RG_HEADS, RG_BLOCK, RG_BLOCK), RG_BLOCK ** -0.5),
        "rg_bx": nrm(ks[12], (DEPTH, RG_HEADS, RG_BLOCK), 0.01),
        "rg_lambda": rg_lambda,
        "hg_lower_bounds": nrm(ks[15], (DEPTH, D_HG), 0.1),
        "hg_out_norm": gain(ks[16], (DEPTH, D_HG)),
        "w_out_a": nrm(ks[17], (DEPTH, D_RG, D_MODEL), D_RG ** -0.5),
        "w_out_b": nrm(ks[18], (DEPTH, D_HG, D_MODEL), D_HG ** -0.5),
        "w_out": nrm(ks[19], (DEPTH, D_MODEL, D_MODEL), D_MODEL ** -0.5),
        "ffn2_norm": gain(ks[20], (DEPTH, D_MODEL)),
        "ffn2_w_gate": nrm(ks[21], (DEPTH, D_MODEL, D_FF), D_MODEL ** -0.5),
        "ffn2_w_up": nrm(ks[22], (DEPTH, D_MODEL, D_FF), D_MODEL ** -0.5),
        "ffn2_w_down": nrm(ks[23], (DEPTH, D_FF, D_MODEL), D_FF ** -0.5),
        "final_norm": gain(ks[24], (D_MODEL,)),
    }


def reference(x, ffn1_norm, ffn1_w_gate, ffn1_w_up, ffn1_w_down, mix_norm, w_in, conv_w, conv_b, rg_wa, rg_ba, rg_wx, rg_bx, rg_lambda, hg_lower_bounds, hg_out_norm, w_out_a, w_out_b, w_out, ffn2_norm, ffn2_w_gate, ffn2_w_up, ffn2_w_down, final_norm):
    lower_bounds = hgrn2_lower_bounds(hg_lower_bounds)
    h = x
    for l in range(DEPTH):
        h = h + FFN_RES_WEIGHT * swiglu(rms_norm(h, ffn1_norm[l]), ffn1_w_gate[l], ffn1_w_up[l], ffn1_w_down[l])
        u = rms_norm(h, mix_norm[l])
        z = u @ w_in[l]
        a_x, a_y, b_q, b_f, b_i, b_g, g_a, g_b = jnp.split(z, SPLIT_POINTS, axis=-1)
        y_a = rglru_branch(a_x, a_y, conv_w[l], conv_b[l], rg_wa[l], rg_ba[l], rg_wx[l], rg_bx[l], rg_lambda[l])
        y_b = hgrn2_branch(b_q, b_f, b_i, b_g, lower_bounds[l], hg_out_norm[l])
        merged = jax.nn.sigmoid(g_a) * (y_a @ w_out_a[l]) + jax.nn.sigmoid(g_b) * (y_b @ w_out_b[l])
        h = h + merged @ w_out[l]
        h = h + FFN_RES_WEIGHT * swiglu(rms_norm(h, ffn2_norm[l]), ffn2_w_gate[l], ffn2_w_up[l], ffn2_w_down[l])
    return rms_norm(h, final_norm)
```

```python
import functools
import math

import numpy as np
import jax
import jax.numpy as jnp
from jax import lax
from jax.experimental import pallas as pl
from jax.experimental.pallas import tpu as pltpu

F32 = jnp.float32
BF16 = jnp.bfloat16

NORM_EPS = 1e-6
FFN_RES_WEIGHT = 0.5
RG_C = 8.0
HG_HEAD_DIM = 128

V7X_VMEM_BYTES = 64 * 1024 * 1024
SUBLANES = 8
LANES = 128

ROW_TILE = 1024
COL_TILE = 512
NORM_ROWS = 128
SEQ_TILE = 512
HG_CHUNK = 64


def _vmem_limit(block_bytes):
    return int(min(V7X_VMEM_BYTES - 4 * 1024 * 1024, block_bytes * 1.3 + 8 * 1024 * 1024))


def _tile(n, pref):
    t = min(pref, n)
    while n % t:
        t //= 2
    return t


def _dot(a, b):
    return jnp.dot(a, b, preferred_element_type=F32)


def _norm_mm_kernel(h_ref, g_ref, *refs, glu):
    if glu:
        w0_ref, w1_ref, o_ref, u_sc = refs
    else:
        w0_ref, o_ref, u_sc = refs

    @pl.when(pl.program_id(1) == 0)
    def _():
        rows = h_ref.shape[0]
        step = min(NORM_ROWS, rows)

        def body(i, c):
            r = pl.ds(pl.multiple_of(i * step, step), step)
            x = h_ref[r, :]
            ms = jnp.mean(x * x, axis=-1, keepdims=True)
            u_sc[r, :] = (x * lax.rsqrt(ms + NORM_EPS) * g_ref[...]).astype(BF16)
            return c

        lax.fori_loop(0, rows // step, body, 0)

    u = u_sc[...]
    if glu:
        g = _dot(u, w0_ref[...])
        up = _dot(u, w1_ref[...])
        o_ref[...] = (g * jax.nn.sigmoid(g) * up).astype(o_ref.dtype)
    else:
        o_ref[...] = _dot(u, w0_ref[...]).astype(o_ref.dtype)


def _norm_mm(h, gain, ws, *, tm, tn):
    T, D = h.shape
    N = ws[0].shape[1]
    glu = len(ws) == 2
    tm = _tile(T, tm)
    tn = _tile(N, tn)
    grid = (T // tm, N // tn)
    w_spec = pl.BlockSpec((D, tn), lambda m, n: (0, n))
    est = 2 * tm * D * 4 + tm * D * 2 + len(ws) * 2 * D * tn * 2 + 2 * tm * tn * 2 + (1 + len(ws)) * tm * tn * 4
    return pl.pallas_call(
        functools.partial(_norm_mm_kernel, glu=glu),
        grid=grid,
        in_specs=[pl.BlockSpec((tm, D), lambda m, n: (m, 0)),
                  pl.BlockSpec((1, D), lambda m, n: (0, 0))] + [w_spec] * len(ws),
        out_specs=pl.BlockSpec((tm, tn), lambda m, n: (m, n)),
        out_shape=jax.ShapeDtypeStruct((T, N), BF16),
        scratch_shapes=[pltpu.VMEM((tm, D), BF16)],
        compiler_params=pltpu.CompilerParams(
            dimension_semantics=("parallel", "arbitrary"),
            vmem_limit_bytes=_vmem_limit(est)),
        name="norm_glu" if glu else "norm_inproj",
    )(h, gain.reshape(1, D), *ws)


def _mm_res_kernel(a_ref, w_ref, h_ref, o_ref, *, scale):
    acc = _dot(a_ref[...], w_ref[...])
    o_ref[...] = h_ref[...] + scale * acc


def _mm_res(a, w, h, scale, *, tm, tn):
    T, K = a.shape
    N = w.shape[1]
    tm = _tile(T, tm)
    tn = _tile(N, tn)
    est = 2 * tm * K * 2 + 2 * K * tn * 2 + 4 * tm * tn * 4 + tm * tn * 4
    return pl.pallas_call(
        functools.partial(_mm_res_kernel, scale=scale),
        grid=(T // tm, N // tn),
        in_specs=[pl.BlockSpec((tm, K), lambda m, n: (m, 0)),
                  pl.BlockSpec((K, tn), lambda m, n: (0, n)),
                  pl.BlockSpec((tm, tn), lambda m, n: (m, n))],
        out_specs=pl.BlockSpec((tm, tn), lambda m, n: (m, n)),
        out_shape=jax.ShapeDtypeStruct((T, N), F32),
        input_output_aliases={2: 0},
        compiler_params=pltpu.CompilerParams(
            dimension_semantics=("parallel", "arbitrary"),
            vmem_limit_bytes=_vmem_limit(est)),
        name="mm_residual",
    )(a, w, h)


def _merge_kernel(ya_ref, yb_ref, wa_ref, wb_ref, ga_ref, gb_ref, o_ref):
    a = _dot(ya_ref[...], wa_ref[...])
    b = _dot(yb_ref[...], wb_ref[...])
    ga = jax.nn.sigmoid(ga_ref[...].astype(F32))
    gb = jax.nn.sigmoid(gb_ref[...].astype(F32))
    o_ref[...] = (ga * a + gb * b).astype(o_ref.dtype)


def _merge(ya, yb, wa, wb, z, ga_col, gb_col, *, tm, tn):
    T, KA = ya.shape
    KB = yb.shape[1]
    N = wa.shape[1]
    tm = _tile(T, tm)
    tn = _tile(N, tn)
    ga_blk, gb_blk = ga_col // tn, gb_col // tn
    assert ga_blk * tn == ga_col and gb_blk * tn == gb_col
    est = 2 * tm * (KA + KB) * 2 + 2 * (KA + KB) * tn * 2 + 6 * tm * tn * 2 + 4 * tm * tn * 4
    return pl.pallas_call(
        _merge_kernel,
        grid=(T // tm, N // tn),
        in_specs=[pl.BlockSpec((tm, KA), lambda m, n: (m, 0)),
                  pl.BlockSpec((tm, KB), lambda m, n: (m, 0)),
                  pl.BlockSpec((KA, tn), lambda m, n: (0, n)),
                  pl.BlockSpec((KB, tn), lambda m, n: (0, n)),
                  pl.BlockSpec((tm, tn), lambda m, n: (m, ga_blk + n)),
                  pl.BlockSpec((tm, tn), lambda m, n: (m, gb_blk + n))],
        out_specs=pl.BlockSpec((tm, tn), lambda m, n: (m, n)),
        out_shape=jax.ShapeDtypeStruct((T, N), BF16),
        compiler_params=pltpu.CompilerParams(
            dimension_semantics=("parallel", "arbitrary"),
            vmem_limit_bytes=_vmem_limit(est)),
        name="gated_merge",
    )(ya, yb, wa, wb, z, z)


def _rmsnorm_kernel(h_ref, g_ref, o_ref):
    x = h_ref[...]
    ms = jnp.mean(x * x, axis=-1, keepdims=True)
    o_ref[...] = x * lax.rsqrt(ms + NORM_EPS) * g_ref[...]


def _rmsnorm(h, gain, *, tr):
    T, D = h.shape
    tr = min(tr, T)
    return pl.pallas_call(
        _rmsnorm_kernel,
        grid=(T // tr,),
        in_specs=[pl.BlockSpec((tr, D), lambda i: (i, 0)),
                  pl.BlockSpec((1, D), lambda i: (0, 0))],
        out_specs=pl.BlockSpec((tr, D), lambda i: (i, 0)),
        out_shape=jax.ShapeDtypeStruct((T, D), F32),
        compiler_params=pltpu.CompilerParams(dimension_semantics=("parallel",)),
        name="final_norm",
    )(h, gain.reshape(1, D))


def _rglru_kernel(x_ref, y_ref, cw_ref, cb_ref, wa_ref, ba_ref, wx_ref, bx_ref, lam_ref,
                  o_ref, xp_sc, a_sc, b_sc, carry_sc, *, conv_width):
    tt, cb = x_ref.shape
    pad = SUBLANES

    @pl.when(pl.program_id(2) == 0)
    def _():
        xp_sc[0:pad, :] = jnp.zeros((pad, cb), F32)
        carry_sc[...] = jnp.zeros((SUBLANES, cb), F32)

    xp_sc[pad:pad + tt, :] = x_ref[...].astype(F32)
    xc = jnp.broadcast_to(cb_ref[0], (tt, cb))
    for k in range(conv_width):
        xc = xc + xp_sc[pl.ds(pad - (conv_width - 1) + k, tt), :] * cw_ref[0, k:k + 1, :]
    xp_sc[0:pad, :] = xp_sc[tt:tt + pad, :]

    xcb = xc.astype(BF16)
    r = jax.nn.sigmoid(_dot(xcb, wa_ref[0]) + ba_ref[0])
    i = jax.nn.sigmoid(_dot(xcb, wx_ref[0]) + bx_ref[0])
    log_a = (RG_C * jax.nn.log_sigmoid(lam_ref[0])) * r
    a = jnp.exp(log_a)
    a_sc[...] = a
    b_sc[...] = jnp.sqrt(-jnp.tanh(log_a) * (a * a + 1.0)) * (i * xc)

    row = lax.broadcasted_iota(jnp.int32, (SUBLANES, cb), 0)

    def group(gi, carry):
        r8 = pl.ds(pl.multiple_of(gi * SUBLANES, SUBLANES), SUBLANES)
        a = a_sc[r8, :]
        b = b_sc[r8, :]
        for d in (1, 2, 4):
            keep = row >= d
            a_prev = jnp.where(keep, pltpu.roll(a, d, 0), 1.0)
            b_prev = jnp.where(keep, pltpu.roll(b, d, 0), 0.0)
            b = a * b_prev + b
            a = a * a_prev
        h = a * carry + b
        b_sc[r8, :] = h
        return jnp.broadcast_to(h[SUBLANES - 1:SUBLANES, :], (SUBLANES, cb))

    carry_sc[...] = lax.fori_loop(0, tt // SUBLANES, group, carry_sc[...], unroll=4)
    o_ref[...] = (b_sc[...] * jax.nn.gelu(y_ref[...].astype(F32))).astype(o_ref.dtype)


def _rglru(z, cw, cb, wa, ba, wx, bx, lam, *, batch, seq, d_rg, tt):
    heads, blk = wa.shape[0], wa.shape[1]
    conv_width = cw.shape[0]
    tt = min(tt, seq)
    nt = seq // tt
    T = batch * seq
    vec = lambda v: v.reshape(heads, 1, blk)
    vspec = pl.BlockSpec((1, 1, blk), lambda b, h, t: (h, 0, 0))
    wspec = pl.BlockSpec((1, blk, blk), lambda b, h, t: (h, 0, 0))
    cwh = cw.reshape(conv_width, heads, blk).transpose(1, 0, 2)
    return pl.pallas_call(
        functools.partial(_rglru_kernel, conv_width=conv_width),
        grid=(batch, heads, nt),
        in_specs=[pl.BlockSpec((tt, blk), lambda b, h, t: (b * nt + t, h)),
                  pl.BlockSpec((tt, blk), lambda b, h, t: (b * nt + t, heads + h)),
                  pl.BlockSpec((1, conv_width, blk), lambda b, h, t: (h, 0, 0)),
                  vspec, wspec, vspec, wspec, vspec, vspec],
        out_specs=pl.BlockSpec((tt, blk), lambda b, h, t: (b * nt + t, h)),
        out_shape=jax.ShapeDtypeStruct((T, d_rg), BF16),
        scratch_shapes=[pltpu.VMEM((tt + 2 * SUBLANES, blk), F32),
                        pltpu.VMEM((tt, blk), F32),
                        pltpu.VMEM((tt, blk), F32),
                        pltpu.VMEM((SUBLANES, blk), F32)],
        compiler_params=pltpu.CompilerParams(
            dimension_semantics=("parallel", "parallel", "arbitrary")),
        name="rglru",
    )(z, z, cwh, vec(cb), wa, vec(ba), wx, vec(bx), vec(lam))


def _hg_constants(chunk):
    levels = int(math.log2(chunk))
    t = np.arange(chunk)[:, None]
    r = np.arange(chunk)[None, :]
    blocks = [(r <= t).astype(np.float32)]
    for lv in range(levels):
        m = chunk >> (lv + 1)
        ref = (t // (2 * m)) * (2 * m) + m - 1
        blocks.append(((r <= t).astype(np.float32) - (r <= ref).astype(np.float32)))
    blocks.append((r > t).astype(np.float32))
    mc = np.concatenate(blocks, axis=0)
    return np.concatenate([mc, mc, mc], axis=1)


def _split3(x):
    hi = x.astype(BF16)
    r1 = x - hi.astype(F32)
    mid = r1.astype(BF16)
    lo = (r1 - mid.astype(F32)).astype(BF16)
    return hi, mid, lo


def _hgrn2_kernel(q_ref, f_ref, i_ref, g_ref, lbc_ref, gain_ref, mc_ref, o_ref, st_sc, *, chunk):
    tt, dk = q_ref.shape
    levels = int(math.log2(chunk))

    @pl.when(pl.program_id(2) == 0)
    def _():
        st_sc[...] = jnp.zeros(st_sc.shape, F32)

    log_lb = lbc_ref[0, 0:1, :]
    log_1m_lb = lbc_ref[0, 1:2, :]
    one_m_lb = lbc_ref[0, 2:3, :]
    gain = gain_ref[0]

    def body(c, carry):
        rows = pl.ds(pl.multiple_of(c * chunk, chunk), chunk)
        zf = f_ref[rows, :].astype(F32)
        e = jnp.exp(-jnp.abs(zf))
        log_sig = jnp.minimum(zf, 0.0) - jnp.log1p(e)
        sig_neg = jnp.where(zf >= 0.0, e, 1.0) / (1.0 + e)
        x2 = log_1m_lb + log_sig
        log_f = jnp.maximum(log_lb, x2) + jnp.log1p(jnp.exp(-jnp.abs(log_lb - x2)))
        kk = one_m_lb * sig_neg
        qp = q_ref[rows, :].astype(F32)
        q = qp * jax.nn.sigmoid(qp)
        v = i_ref[rows, :]

        lf3 = jnp.concatenate(_split3(log_f), axis=0)
        x = _dot(mc_ref[...], lf3)
        bcum = x[0:chunk]
        d_dec = x[(levels + 1) * chunk:(levels + 2) * chunk]

        row = lax.broadcasted_iota(jnp.int32, (chunk, dk), 0)
        rr = lax.broadcasted_iota(jnp.int32, (chunk, chunk), 0)
        cc = lax.broadcasted_iota(jnp.int32, (chunk, chunk), 1)
        nt_dims = (((1,), (1,)), ((), ()))
        p = jnp.where(rr == cc, lax.dot_general(q.astype(BF16), kk.astype(BF16), nt_dims,
                                                preferred_element_type=F32), 0.0)
        for lv in range(levels):
            m = chunk >> (lv + 1)
            w = jnp.exp(-jnp.abs(x[(lv + 1) * chunk:(lv + 2) * chunk]))
            right = (row & m) != 0
            qe = jnp.where(right, q * w, 0.0).astype(BF16)
            ke = jnp.where(right, 0.0, kk * w).astype(BF16)
            s = lax.dot_general(qe, ke, nt_dims, preferred_element_type=F32)
            p = p + jnp.where((rr ^ cc) < 2 * m, s, 0.0)

        st = st_sc[...]
        qd = (q * jnp.exp(bcum)).astype(BF16)
        o = _dot(p.astype(BF16), v) + lax.dot_general(qd, st.astype(BF16), nt_dims,
                                                      preferred_element_type=F32)
        k_dec = (kk * jnp.exp(d_dec)).astype(BF16)
        upd = lax.dot_general(v, k_dec, (((0,), (0,)), ((), ())), preferred_element_type=F32)
        st_sc[...] = st * jnp.exp(bcum[chunk - 1:chunk, :]) + upd

        o = o * lax.rsqrt(jnp.mean(o * o, axis=-1, keepdims=True) + NORM_EPS) * gain
        gp = g_ref[rows, :].astype(F32)
        o_ref[rows, :] = (o * (gp * jax.nn.sigmoid(gp))).astype(o_ref.dtype)
        return carry

    lax.fori_loop(0, tt // chunk, body, 0)


def _hgrn2(z, lbc, gain, *, batch, seq, d_hg, col0, tt, chunk):
    dk = HG_HEAD_DIM
    heads = d_hg // dk
    tt = min(tt, seq)
    chunk = min(chunk, tt)
    nt = seq // tt
    T = batch * seq
    mc = jnp.asarray(_hg_constants(chunk), BF16)
    base = col0 // dk

    def zspec(j):
        return pl.BlockSpec((tt, dk), lambda b, h, t: (b * nt + t, base + j * heads + h))

    return pl.pallas_call(
        functools.partial(_hgrn2_kernel, chunk=chunk),
        grid=(batch, heads, nt),
        in_specs=[zspec(0), zspec(1), zspec(2), zspec(3),
                  pl.BlockSpec((1, 3, dk), lambda b, h, t: (h, 0, 0)),
                  pl.BlockSpec((1, 1, dk), lambda b, h, t: (h, 0, 0)),
                  pl.BlockSpec(mc.shape, lambda b, h, t: (0, 0))],
        out_specs=pl.BlockSpec((tt, dk), lambda b, h, t: (b * nt + t, h)),
        out_shape=jax.ShapeDtypeStruct((T, d_hg), BF16),
        scratch_shapes=[pltpu.VMEM((dk, dk), F32)],
        compiler_params=pltpu.CompilerParams(
            dimension_semantics=("parallel", "parallel", "arbitrary")),
        name="hgrn2",
    )(z, z, z, z, lbc, gain.reshape(heads, 1, dk), mc)


def _lb_kernel(p_ref, o_ref):
    depth = p_ref.shape[0]
    p = p_ref[...]
    mx = jnp.max(p, axis=0, keepdims=True)
    ex = jnp.exp(p - mx)
    sm = ex / jnp.sum(ex, axis=0, keepdims=True)
    c0 = sm[0:1, :]
    c = c0
    for l in range(depth):
        if l > 0:
            c = c + sm[l:l + 1, :]
        lb = c - c0
        o_ref[l, 0:1, :] = jnp.log(lb)
        o_ref[l, 1:2, :] = jnp.log1p(-lb)
        o_ref[l, 2:3, :] = 1.0 - lb


def _lb_constants(lb_param):
    depth, d_hg = lb_param.shape
    return pl.pallas_call(
        _lb_kernel,
        out_shape=jax.ShapeDtypeStruct((depth, 3, d_hg), F32),
        name="hgrn2_lower_bounds",
    )(lb_param)


def kernel(x, ffn1_norm, ffn1_w_gate, ffn1_w_up, ffn1_w_down, mix_norm, w_in, conv_w, conv_b, rg_wa, rg_ba, rg_wx, rg_bx, rg_lambda, hg_lower_bounds, hg_out_norm, w_out_a, w_out_b, w_out, ffn2_norm, ffn2_w_gate, ffn2_w_up, ffn2_w_down, final_norm):
    batch, seq, d_model = x.shape
    depth = w_in.shape[0]
    d_rg = rg_lambda.shape[1]
    d_hg = hg_lower_bounds.shape[1]
    heads_hg = d_hg // HG_HEAD_DIM
    T = batch * seq
    tm, tn = ROW_TILE, COL_TILE

    lbc = _lb_constants(hg_lower_bounds)
    lbc = lbc.reshape(depth, 3, heads_hg, HG_HEAD_DIM).transpose(0, 2, 1, 3)

    bf = lambda w: w.astype(BF16)
    h = x.reshape(T, d_model)
    for l in range(depth):
        act = _norm_mm(h, ffn1_norm[l], [bf(ffn1_w_gate[l]), bf(ffn1_w_up[l])], tm=tm // 2, tn=tn)
        h = _mm_res(act, bf(ffn1_w_down[l]), h, FFN_RES_WEIGHT, tm=tm, tn=tn)

        z = _norm_mm(h, mix_norm[l], [bf(w_in[l])], tm=tm // 2, tn=2 * tn)
        y_a = _rglru(z, conv_w[l], conv_b[l], bf(rg_wa[l]), rg_ba[l], bf(rg_wx[l]), rg_bx[l],
                     rg_lambda[l], batch=batch, seq=seq, d_rg=d_rg, tt=SEQ_TILE)
        y_b = _hgrn2(z, lbc[l], hg_out_norm[l], batch=batch, seq=seq, d_hg=d_hg,
                     col0=2 * d_rg, tt=SEQ_TILE, chunk=HG_CHUNK)
        merged = _merge(y_a, y_b, bf(w_out_a[l]), bf(w_out_b[l]), z,
                        2 * d_rg + 4 * d_hg, 2 * d_rg + 4 * d_hg + d_model, tm=tm, tn=tn)
        h = _mm_res(merged, bf(w_out[l]), h, 1.0, tm=tm, tn=tn)

        act = _norm_mm(h, ffn2_norm[l], [bf(ffn2_w_gate[l]), bf(ffn2_w_up[l])], tm=tm // 2, tn=tn)
        h = _mm_res(act, bf(ffn2_w_down[l]), h, FFN_RES_WEIGHT, tm=tm, tn=tn)
    out = _rmsnorm(h, final_norm, tr=256)
    return out.reshape(batch, seq, d_model)
```

```python
import functools
import math

import numpy as np
import jax
import jax.numpy as jnp
from jax import lax
from jax.experimental import pallas as pl
from jax.experimental.pallas import tpu as pltpu

F32 = jnp.float32
BF16 = jnp.bfloat16

NORM_EPS = 1e-6
FFN_RES_WEIGHT = 0.5
RG_C = 8.0
HG_HEAD_DIM = 128

V7X_VMEM_BYTES = 64 * 1024 * 1024
SUBLANES = 8
LANES = 128

ROW_TILE = 1024
COL_TILE = 512
NORM_ROWS = 128
SEQ_TILE = 512
HG_SEQ_TILE = 1024
HG_SUB = 512
HG_CHUNK = 64

LOG2E = 1.4426950408889634
NT_DIMS = (((1,), (1,)), ((), ()))
TN_DIMS = (((0,), (0,)), ((), ()))


def _vmem_limit(block_bytes):
    return int(min(V7X_VMEM_BYTES - 4 * 1024 * 1024, block_bytes * 1.3 + 8 * 1024 * 1024))


def _tile(n, pref):
    t = min(pref, n)
    while n % t:
        t //= 2
    return t


def _dot(a, b):
    return jnp.dot(a, b, preferred_element_type=F32)


def _dot_dims(a, b, dims):
    return lax.dot_general(a, b, dims, preferred_element_type=F32)


def _norm_mm_kernel(h_ref, g_ref, *refs, glu):
    if glu:
        w0_ref, w1_ref, o_ref, u_sc = refs
    else:
        w0_ref, o_ref, u_sc = refs

    @pl.when(pl.program_id(1) == 0)
    def _():
        rows = h_ref.shape[0]
        step = min(NORM_ROWS, rows)

        def body(i, c):
            r = pl.ds(pl.multiple_of(i * step, step), step)
            x = h_ref[r, :]
            ms = jnp.mean(x * x, axis=-1, keepdims=True)
            u_sc[r, :] = (x * lax.rsqrt(ms + NORM_EPS) * g_ref[...]).astype(BF16)
            return c

        lax.fori_loop(0, rows // step, body, 0)

    u = u_sc[...]
    if glu:
        g = _dot(u, w0_ref[...])
        up = _dot(u, w1_ref[...])
        o_ref[...] = (g * jax.nn.sigmoid(g) * up).astype(o_ref.dtype)
    else:
        o_ref[...] = _dot(u, w0_ref[...]).astype(o_ref.dtype)


def _norm_mm(h, gain, ws, *, tm, tn):
    T, D = h.shape
    N = ws[0].shape[1]
    glu = len(ws) == 2
    tm = _tile(T, tm)
    tn = _tile(N, tn)
    grid = (T // tm, N // tn)
    w_spec = pl.BlockSpec((D, tn), lambda m, n: (0, n))
    est = 2 * tm * D * 4 + tm * D * 2 + len(ws) * 2 * D * tn * 2 + 2 * tm * tn * 2 + (1 + len(ws)) * tm * tn * 4
    return pl.pallas_call(
        functools.partial(_norm_mm_kernel, glu=glu),
        grid=grid,
        in_specs=[pl.BlockSpec((tm, D), lambda m, n: (m, 0)),
                  pl.BlockSpec((1, D), lambda m, n: (0, 0))] + [w_spec] * len(ws),
        out_specs=pl.BlockSpec((tm, tn), lambda m, n: (m, n)),
        out_shape=jax.ShapeDtypeStruct((T, N), BF16),
        scratch_shapes=[pltpu.VMEM((tm, D), BF16)],
        compiler_params=pltpu.CompilerParams(
            dimension_semantics=("parallel", "arbitrary"),
            vmem_limit_bytes=_vmem_limit(est)),
        name="norm_glu" if glu else "norm_inproj",
    )(h, gain.reshape(1, D), *ws)


def _mm_res_kernel(a_ref, w_ref, h_ref, o_ref, *, scale):
    acc = _dot(a_ref[...], w_ref[...])
    o_ref[...] = h_ref[...] + scale * acc


def _mm_res(a, w, h, scale, *, tm, tn):
    T, K = a.shape
    N = w.shape[1]
    tm = _tile(T, tm)
    tn = _tile(N, tn)
    est = 2 * tm * K * 2 + 2 * K * tn * 2 + 4 * tm * tn * 4 + tm * tn * 4
    return pl.pallas_call(
        functools.partial(_mm_res_kernel, scale=scale),
        grid=(T // tm, N // tn),
        in_specs=[pl.BlockSpec((tm, K), lambda m, n: (m, 0)),
                  pl.BlockSpec((K, tn), lambda m, n: (0, n)),
                  pl.BlockSpec((tm, tn), lambda m, n: (m, n))],
        out_specs=pl.BlockSpec((tm, tn), lambda m, n: (m, n)),
        out_shape=jax.ShapeDtypeStruct((T, N), F32),
        input_output_aliases={2: 0},
        compiler_params=pltpu.CompilerParams(
            dimension_semantics=("parallel", "arbitrary"),
            vmem_limit_bytes=_vmem_limit(est)),
        name="mm_residual",
    )(a, w, h)


def _merge_kernel(ya_ref, yb_ref, wa_ref, wb_ref, ga_ref, gb_ref, o_ref):
    a = _dot(ya_ref[...], wa_ref[...])
    b = _dot(yb_ref[...], wb_ref[...])
    ga = jax.nn.sigmoid(ga_ref[...].astype(F32))
    gb = jax.nn.sigmoid(gb_ref[...].astype(F32))
    o_ref[...] = (ga * a + gb * b).astype(o_ref.dtype)


def _merge(ya, yb, wa, wb, z, ga_col, gb_col, *, tm, tn):
    T, KA = ya.shape
    KB = yb.shape[1]
    N = wa.shape[1]
    tm = _tile(T, tm)
    tn = _tile(N, tn)
    ga_blk, gb_blk = ga_col // tn, gb_col // tn
    assert ga_blk * tn == ga_col and gb_blk * tn == gb_col
    est = 2 * tm * (KA + KB) * 2 + 2 * (KA + KB) * tn * 2 + 6 * tm * tn * 2 + 4 * tm * tn * 4
    return pl.pallas_call(
        _merge_kernel,
        grid=(T // tm, N // tn),
        in_specs=[pl.BlockSpec((tm, KA), lambda m, n: (m, 0)),
                  pl.BlockSpec((tm, KB), lambda m, n: (m, 0)),
                  pl.BlockSpec((KA, tn), lambda m, n: (0, n)),
                  pl.BlockSpec((KB, tn), lambda m, n: (0, n)),
                  pl.BlockSpec((tm, tn), lambda m, n: (m, ga_blk + n)),
                  pl.BlockSpec((tm, tn), lambda m, n: (m, gb_blk + n))],
        out_specs=pl.BlockSpec((tm, tn), lambda m, n: (m, n)),
        out_shape=jax.ShapeDtypeStruct((T, N), BF16),
        compiler_params=pltpu.CompilerParams(
            dimension_semantics=("parallel", "arbitrary"),
            vmem_limit_bytes=_vmem_limit(est)),
        name="gated_merge",
    )(ya, yb, wa, wb, z, z)


def _rmsnorm_kernel(h_ref, g_ref, o_ref):
    x = h_ref[...]
    ms = jnp.mean(x * x, axis=-1, keepdims=True)
    o_ref[...] = x * lax.rsqrt(ms + NORM_EPS) * g_ref[...]


def _rmsnorm(h, gain, *, tr):
    T, D = h.shape
    tr = min(tr, T)
    return pl.pallas_call(
        _rmsnorm_kernel,
        grid=(T // tr,),
        in_specs=[pl.BlockSpec((tr, D), lambda i: (i, 0)),
                  pl.BlockSpec((1, D), lambda i: (0, 0))],
        out_specs=pl.BlockSpec((tr, D), lambda i: (i, 0)),
        out_shape=jax.ShapeDtypeStruct((T, D), F32),
        compiler_params=pltpu.CompilerParams(dimension_semantics=("parallel",)),
        name="final_norm",
    )(h, gain.reshape(1, D))


def _rglru_kernel(x_ref, y_ref, cw_ref, cb_ref, wa_ref, ba_ref, wx_ref, bx_ref, lam_ref,
                  o_ref, xp_sc, a_sc, b_sc, carry_sc, *, conv_width):
    tt, cb = x_ref.shape
    pad = SUBLANES

    @pl.when(pl.program_id(2) == 0)
    def _():
        xp_sc[0:pad, :] = jnp.zeros((pad, cb), F32)
        carry_sc[...] = jnp.zeros((SUBLANES, cb), F32)

    xp_sc[pad:pad + tt, :] = x_ref[...].astype(F32)
    xc = jnp.broadcast_to(cb_ref[0], (tt, cb))
    for k in range(conv_width):
        xc = xc + xp_sc[pl.ds(pad - (conv_width - 1) + k, tt), :] * cw_ref[0, k:k + 1, :]
    xp_sc[0:pad, :] = xp_sc[tt:tt + pad, :]

    xcb = xc.astype(BF16)
    r = jax.nn.sigmoid(_dot(xcb, wa_ref[0]) + ba_ref[0])
    i = jax.nn.sigmoid(_dot(xcb, wx_ref[0]) + bx_ref[0])
    log_a = (RG_C * jax.nn.log_sigmoid(lam_ref[0])) * r
    a = jnp.exp(log_a)
    a_sc[...] = a
    b_sc[...] = jnp.sqrt(-jnp.tanh(log_a) * (a * a + 1.0)) * (i * xc)

    row = lax.broadcasted_iota(jnp.int32, (SUBLANES, cb), 0)

    def group(gi, carry):
        r8 = pl.ds(pl.multiple_of(gi * SUBLANES, SUBLANES), SUBLANES)
        a = a_sc[r8, :]
        b = b_sc[r8, :]
        for d in (1, 2, 4):
            keep = row >= d
            a_prev = jnp.where(keep, pltpu.roll(a, d, 0), 1.0)
            b_prev = jnp.where(keep, pltpu.roll(b, d, 0), 0.0)
            b = a * b_prev + b
            a = a * a_prev
        h = a * carry + b
        b_sc[r8, :] = h
        return jnp.broadcast_to(h[SUBLANES - 1:SUBLANES, :], (SUBLANES, cb))

    carry_sc[...] = lax.fori_loop(0, tt // SUBLANES, group, carry_sc[...], unroll=4)
    o_ref[...] = (b_sc[...] * jax.nn.gelu(y_ref[...].astype(F32))).astype(o_ref.dtype)


def _rglru(z, cw, cb, wa, ba, wx, bx, lam, *, batch, seq, d_rg, tt):
    heads, blk = wa.shape[0], wa.shape[1]
    conv_width = cw.shape[0]
    tt = min(tt, seq)
    nt = seq // tt
    T = batch * seq
    vec = lambda v: v.reshape(heads, 1, blk)
    vspec = pl.BlockSpec((1, 1, blk), lambda b, h, t: (h, 0, 0))
    wspec = pl.BlockSpec((1, blk, blk), lambda b, h, t: (h, 0, 0))
    cwh = cw.reshape(conv_width, heads, blk).transpose(1, 0, 2)
    return pl.pallas_call(
        functools.partial(_rglru_kernel, conv_width=conv_width),
        grid=(batch, heads, nt),
        in_specs=[pl.BlockSpec((tt, blk), lambda b, h, t: (b * nt + t, h)),
                  pl.BlockSpec((tt, blk), lambda b, h, t: (b * nt + t, heads + h)),
                  pl.BlockSpec((1, conv_width, blk), lambda b, h, t: (h, 0, 0)),
                  vspec, wspec, vspec, wspec, vspec, vspec],
        out_specs=pl.BlockSpec((tt, blk), lambda b, h, t: (b * nt + t, h)),
        out_shape=jax.ShapeDtypeStruct((T, d_rg), BF16),
        scratch_shapes=[pltpu.VMEM((tt + 2 * SUBLANES, blk), F32),
                        pltpu.VMEM((tt, blk), F32),
                        pltpu.VMEM((tt, blk), F32),
                        pltpu.VMEM((SUBLANES, blk), F32)],
        compiler_params=pltpu.CompilerParams(
            dimension_semantics=("parallel", "parallel", "arbitrary")),
        name="rglru",
    )(z, z, cwh, vec(cb), wa, vec(ba), wx, vec(bx), vec(lam))


def _hg_constants(chunk):
    t = np.arange(chunk)[:, None]
    r = np.arange(chunk)[None, :]
    tril = (r <= t).astype(np.float32)
    return np.concatenate([tril, tril, tril], axis=1)


def _split3(x):
    hi = x.astype(BF16)
    r1 = x - hi.astype(F32)
    mid = r1.astype(BF16)
    lo = (r1 - mid.astype(F32)).astype(BF16)
    return hi, mid, lo


def _hg_level_operands(q, kk, fd, b2, chunk):
    dk = q.shape[1]
    zeros = lambda n: jnp.zeros((n, dk), F32)
    qes, kes, halves = [q], [kk], [0]
    m = chunk // 2
    while m >= SUBLANES:
        qparts, kparts = [], []
        for s in range(0, chunk, 2 * m):
            ref = b2[s + m - 1:s + m, :]
            qparts += [zeros(m), q[s + m:s + 2 * m] * jnp.exp2(b2[s + m:s + 2 * m] - ref)]
            kparts += [kk[s:s + m] * jnp.exp2(ref - b2[s:s + m]), zeros(m)]
        qes.append(jnp.concatenate(qparts, axis=0))
        kes.append(jnp.concatenate(kparts, axis=0))
        halves.append(m)
        m //= 2
    row = lax.broadcasted_iota(jnp.int32, (chunk, dk), 0)
    b3 = b2.reshape(chunk // SUBLANES, SUBLANES, dk)
    ref = jnp.broadcast_to(b3[:, 3:4, :], b3.shape).reshape(chunk, dk)
    w = jnp.exp2(-jnp.abs(b2 - ref))
    upper = (row & 4) != 0
    qes.append(jnp.where(upper, q * w, 0.0))
    kes.append(jnp.where(upper, 0.0, kk * w))
    halves.append(4)
    f_prev = pltpu.roll(fd, 1, 0)
    f_next = pltpu.roll(fd, chunk - 1, 0)
    r4 = row & 3
    qes.append(jnp.where(r4 >= 2, q * jnp.where(r4 == 3, fd * f_prev, fd), 0.0))
    kes.append(jnp.where(r4 >= 2, 0.0, kk * jnp.where(r4 == 0, f_next, 1.0)))
    halves.append(2)
    odd = (row & 1) != 0
    qes.append(jnp.where(odd, q * fd, 0.0))
    kes.append(jnp.where(odd, 0.0, kk))
    halves.append(1)
    return qes, kes, halves


def _hgrn2_kernel(q_ref, f_ref, i_ref, g_ref, lbc_ref, gain_ref, mc_ref, o_ref, st_sc, *, chunk, sub):
    tt, dk = q_ref.shape
    nch = sub // chunk

    @pl.when(pl.program_id(2) == 0)
    def _():
        st_sc[...] = jnp.zeros(st_sc.shape, F32)

    log_lb = lbc_ref[0, 0:1, :]
    log_1m_lb = lbc_ref[0, 1:2, :]
    one_m_lb = lbc_ref[0, 2:3, :]
    lb = lbc_ref[0, 3:4, :]
    gain = gain_ref[0]

    def body(j, carry):
        rows = pl.ds(pl.multiple_of(j * sub, sub), sub)
        zf = f_ref[rows, :].astype(F32)
        e = jnp.exp2(jnp.abs(zf) * (-LOG2E))
        den = 1.0 + e
        inv = 1.0 / den
        pos = zf >= 0.0
        log_sig = jnp.minimum(zf, 0.0) - jnp.log(den)
        einv = e * inv
        sig_pos = jnp.where(pos, inv, einv)
        sig_neg = jnp.where(pos, einv, inv)
        x2 = log_1m_lb + log_sig
        lf2 = (jnp.maximum(log_lb, x2)
               + jnp.log(1.0 + jnp.exp2(jnp.abs(log_lb - x2) * (-LOG2E)))) * LOG2E
        fd = lb + one_m_lb * sig_pos
        kk = one_m_lb * sig_neg
        qp = q_ref[rows, :].astype(F32)
        q = qp * jax.nn.sigmoid(qp)
        v = i_ref[rows, :]

        w3 = jnp.concatenate(
            [jnp.concatenate([p[c * chunk:(c + 1) * chunk] for c in range(nch)], axis=1)
             for p in _split3(lf2)], axis=0)
        xw = _dot(mc_ref[...], w3)

        xr = (lax.broadcasted_iota(jnp.int32, (chunk, chunk), 0)
              ^ lax.broadcasted_iota(jnp.int32, (chunk, chunk), 1))
        ps, qds, upds, decs = [], [], [], []
        for c in range(nch):
            sl = slice(c * chunk, (c + 1) * chunk)
            b2 = xw[:, c * dk:(c + 1) * dk]
            qc, kc = q[sl], kk[sl]
            qes, kes, halves = _hg_level_operands(qc, kc, fd[sl], b2, chunk)
            p = jnp.where(xr == 0, _dot_dims(qes[0].astype(BF16), kes[0].astype(BF16), NT_DIMS), 0.0)
            for lv in range(1, len(halves)):
                s = _dot_dims(qes[lv].astype(BF16), kes[lv].astype(BF16), NT_DIMS)
                p = jnp.where((xr >> int(math.log2(halves[lv]))) == 1, s, p)
            ps.append(p.astype(BF16))
            qds.append((qc * jnp.exp2(b2)).astype(BF16))
            b_last = b2[chunk - 1:chunk, :]
            k_dec = (kc * jnp.exp2(b_last - b2)).astype(BF16)
            upds.append(_dot_dims(v[sl], k_dec, TN_DIMS))
            decs.append(jnp.exp2(b_last))
        st = st_sc[...]
        outs = []
        for c in range(nch):
            sl = slice(c * chunk, (c + 1) * chunk)
            outs.append(_dot(ps[c], v[sl]) + _dot_dims(qds[c], st.astype(BF16), NT_DIMS))
            st = st * decs[c] + upds[c]
        st_sc[...] = st

        o = jnp.concatenate(outs, axis=0)
        o = o * lax.rsqrt(jnp.mean(o * o, axis=-1, keepdims=True) + NORM_EPS) * gain
        gp = g_ref[rows, :].astype(F32)
        o_ref[rows, :] = (o * (gp * jax.nn.sigmoid(gp))).astype(o_ref.dtype)
        return carry

    lax.fori_loop(0, tt // sub, body, 0)


def _hgrn2(z, lbc, gain, *, batch, seq, d_hg, col0, tt, sub, chunk):
    dk = HG_HEAD_DIM
    heads = d_hg // dk
    tt = min(tt, seq)
    sub = min(sub, tt)
    chunk = min(chunk, sub)
    assert chunk >= 2 * SUBLANES and tt % sub == 0 and sub % chunk == 0
    nt = seq // tt
    T = batch * seq
    mc = jnp.asarray(_hg_constants(chunk), BF16)
    base = col0 // dk

    def zspec(j):
        return pl.BlockSpec((tt, dk), lambda b, h, t: (b * nt + t, base + j * heads + h))

    return pl.pallas_call(
        functools.partial(_hgrn2_kernel, chunk=chunk, sub=sub),
        grid=(batch, heads, nt),
        in_specs=[zspec(0), zspec(1), zspec(2), zspec(3),
                  pl.BlockSpec((1, 4, dk), lambda b, h, t: (h, 0, 0)),
                  pl.BlockSpec((1, 1, dk), lambda b, h, t: (h, 0, 0)),
                  pl.BlockSpec(mc.shape, lambda b, h, t: (0, 0))],
        out_specs=pl.BlockSpec((tt, dk), lambda b, h, t: (b * nt + t, h)),
        out_shape=jax.ShapeDtypeStruct((T, d_hg), BF16),
        scratch_shapes=[pltpu.VMEM((dk, dk), F32)],
        compiler_params=pltpu.CompilerParams(
            dimension_semantics=("parallel", "parallel", "arbitrary")),
        name="hgrn2",
    )(z, z, z, z, lbc, gain.reshape(heads, 1, dk), mc)


def _lb_kernel(p_ref, o_ref):
    depth = p_ref.shape[0]
    p = p_ref[...]
    mx = jnp.max(p, axis=0, keepdims=True)
    ex = jnp.exp(p - mx)
    sm = ex / jnp.sum(ex, axis=0, keepdims=True)
    c0 = sm[0:1, :]
    c = c0
    for l in range(depth):
        if l > 0:
            c = c + sm[l:l + 1, :]
        lb = c - c0
        o_ref[l, 0:1, :] = jnp.log(lb)
        o_ref[l, 1:2, :] = jnp.log1p(-lb)
        o_ref[l, 2:3, :] = 1.0 - lb
        o_ref[l, 3:4, :] = lb


def _lb_constants(lb_param):
    depth, d_hg = lb_param.shape
    return pl.pallas_call(
        _lb_kernel,
        out_shape=jax.ShapeDtypeStruct((depth, 4, d_hg), F32),
        name="hgrn2_lower_bounds",
    )(lb_param)


def kernel(x, ffn1_norm, ffn1_w_gate, ffn1_w_up, ffn1_w_down, mix_norm, w_in, conv_w, conv_b, rg_wa, rg_ba, rg_wx, rg_bx, rg_lambda, hg_lower_bounds, hg_out_norm, w_out_a, w_out_b, w_out, ffn2_norm, ffn2_w_gate, ffn2_w_up, ffn2_w_down, final_norm):
    batch, seq, d_model = x.shape
    depth = w_in.shape[0]
    d_rg = rg_lambda.shape[1]
    d_hg = hg_lower_bounds.shape[1]
    heads_hg = d_hg // HG_HEAD_DIM
    T = batch * seq
    tm, tn = ROW_TILE, COL_TILE

    lbc = _lb_constants(hg_lower_bounds)
    lbc = lbc.reshape(depth, 4, heads_hg, HG_HEAD_DIM).transpose(0, 2, 1, 3)

    bf = lambda w: w.astype(BF16)
    h = x.reshape(T, d_model)
    for l in range(depth):
        act = _norm_mm(h, ffn1_norm[l], [bf(ffn1_w_gate[l]), bf(ffn1_w_up[l])], tm=tm // 2, tn=tn)
        h = _mm_res(act, bf(ffn1_w_down[l]), h, FFN_RES_WEIGHT, tm=tm, tn=tn)

        z = _norm_mm(h, mix_norm[l], [bf(w_in[l])], tm=tm // 2, tn=2 * tn)
        y_a = _rglru(z, conv_w[l], conv_b[l], bf(rg_wa[l]), rg_ba[l], bf(rg_wx[l]), rg_bx[l],
                     rg_lambda[l], batch=batch, seq=seq, d_rg=d_rg, tt=SEQ_TILE)
        y_b = _hgrn2(z, lbc[l], hg_out_norm[l], batch=batch, seq=seq, d_hg=d_hg,
                     col0=2 * d_rg, tt=HG_SEQ_TILE, sub=HG_SUB, chunk=HG_CHUNK)
        merged = _merge(y_a, y_b, bf(w_out_a[l]), bf(w_out_b[l]), z,
                        2 * d_rg + 4 * d_hg, 2 * d_rg + 4 * d_hg + d_model, tm=tm, tn=tn)
        h = _mm_res(merged, bf(w_out[l]), h, 1.0, tm=tm, tn=tn)

        act = _norm_mm(h, ffn2_norm[l], [bf(ffn2_w_gate[l]), bf(ffn2_w_up[l])], tm=tm // 2, tn=tn)
        h = _mm_res(act, bf(ffn2_w_down[l]), h, FFN_RES_WEIGHT, tm=tm, tn=tn)
    out = _rmsnorm(h, final_norm, tr=256)
    return out.reshape(batch, seq, d_model)
```

```python
import functools
import math

import numpy as np
import jax
import jax.numpy as jnp
from jax import lax
from jax.experimental import pallas as pl
from jax.experimental.pallas import tpu as pltpu

F32 = jnp.float32
BF16 = jnp.bfloat16

NORM_EPS = 1e-6
FFN_RES_WEIGHT = 0.5
RG_C = 8.0
HG_HEAD_DIM = 128

V7X_VMEM_BYTES = 64 * 1024 * 1024
SUBLANES = 8
LANES = 128
BF16_ROWS = 16

ROW_TILE = 1024
COL_TILE = 512
NORM_ROWS = 128
SEQ_TILE = 512
HG_SEQ_TILE = 1024
HG_SUB = 512
HG_CHUNK = 64

LOG2E = 1.4426950408889634
NT_DIMS = (((1,), (1,)), ((), ()))
TN_DIMS = (((0,), (0,)), ((), ()))


def _vmem_limit(block_bytes):
    return int(min(V7X_VMEM_BYTES - 4 * 1024 * 1024, block_bytes * 1.3 + 8 * 1024 * 1024))


def _tile(n, pref):
    t = min(pref, n)
    while n % t:
        t //= 2
    return t


def _dot(a, b):
    return jnp.dot(a, b, preferred_element_type=F32)


def _dot_dims(a, b, dims):
    return lax.dot_general(a, b, dims, preferred_element_type=F32)


def _cast_rows(rows, steps):
    pr = BF16_ROWS
    while rows % pr or rows // pr > steps:
        pr += BF16_ROWS
    return pr


def _cast_plan(casts, grid):
    steps = grid[0] * grid[1]
    in_specs, out_specs, out_shapes, operands = [], [], [], []
    for stack, layer in casts:
        _, R, C = stack.shape
        pr = _cast_rows(R, steps)
        pieces = R // pr

        def piece(m, n, pieces=pieces):
            return jnp.minimum(m * grid[1] + n, pieces - 1)

        in_specs.append(pl.BlockSpec((None, pr, C), lambda m, n, layer=layer, piece=piece: (layer, piece(m, n), 0)))
        out_specs.append(pl.BlockSpec((pr, C), lambda m, n, piece=piece: (piece(m, n), 0)))
        out_shapes.append(jax.ShapeDtypeStruct((R, C), BF16))
        operands.append(stack)
    return in_specs, out_specs, out_shapes, operands


def _cast_bytes(casts, grid):
    steps = grid[0] * grid[1]
    total = 0
    for stack, _ in casts:
        _, R, C = stack.shape
        total += 2 * _cast_rows(R, steps) * C * (4 + 2)
    return total


def _do_casts(src_refs, dst_refs):
    for s, d in zip(src_refs, dst_refs):
        d[...] = s[...].astype(BF16)


def _split_refs(refs, n_in, n_cast):
    ins = refs[:n_in]
    cast_src = refs[n_in:n_in + n_cast]
    out = refs[n_in + n_cast]
    cast_dst = refs[n_in + n_cast + 1:n_in + 2 * n_cast + 1]
    scratch = refs[n_in + 2 * n_cast + 1:]
    return ins, cast_src, out, cast_dst, scratch


def _norm_mm_kernel(*refs, n_w, n_cast):
    (h_ref, g_ref, *w_refs), cast_src, o_ref, cast_dst, (u_sc,) = _split_refs(refs, 2 + n_w, n_cast)

    @pl.when(pl.program_id(1) == 0)
    def _():
        rows = h_ref.shape[0]
        step = min(NORM_ROWS, rows)

        def body(i, c):
            r = pl.ds(pl.multiple_of(i * step, step), step)
            x = h_ref[r, :]
            ms = jnp.mean(x * x, axis=-1, keepdims=True)
            u_sc[r, :] = (x * lax.rsqrt(ms + NORM_EPS) * g_ref[...]).astype(BF16)
            return c

        lax.fori_loop(0, rows // step, body, 0)

    u = u_sc[...]
    if n_w == 2:
        g = _dot(u, w_refs[0][...])
        up = _dot(u, w_refs[1][...])
        o_ref[...] = (g * jax.nn.sigmoid(g) * up).astype(o_ref.dtype)
    else:
        o_ref[...] = _dot(u, w_refs[0][...]).astype(o_ref.dtype)
    _do_casts(cast_src, cast_dst)


def _norm_mm(h, gain, ws, casts, *, tm, tn):
    T, D = h.shape
    N = ws[0].shape[1]
    tm = _tile(T, tm)
    tn = _tile(N, tn)
    grid = (T // tm, N // tn)
    w_spec = pl.BlockSpec((D, tn), lambda m, n: (0, n))
    c_in, c_out, c_shapes, c_ops = _cast_plan(casts, grid)
    est = (2 * tm * D * 4 + tm * D * 2 + len(ws) * 2 * D * tn * 2 + 2 * tm * tn * 2
           + (1 + len(ws)) * tm * tn * 4 + _cast_bytes(casts, grid))
    outs = pl.pallas_call(
        functools.partial(_norm_mm_kernel, n_w=len(ws), n_cast=len(casts)),
        grid=grid,
        in_specs=[pl.BlockSpec((tm, D), lambda m, n: (m, 0)),
                  pl.BlockSpec((1, D), lambda m, n: (0, 0))] + [w_spec] * len(ws) + c_in,
        out_specs=[pl.BlockSpec((tm, tn), lambda m, n: (m, n))] + c_out,
        out_shape=[jax.ShapeDtypeStruct((T, N), BF16)] + c_shapes,
        scratch_shapes=[pltpu.VMEM((tm, D), BF16)],
        compiler_params=pltpu.CompilerParams(
            dimension_semantics=("arbitrary", "arbitrary"),
            vmem_limit_bytes=_vmem_limit(est)),
        name="norm_glu" if len(ws) == 2 else "norm_inproj",
    )(h, gain.reshape(1, D), *ws, *c_ops)
    return outs[0], outs[1:]


def _mm_res_kernel(*refs, scale, n_cast):
    (a_ref, w_ref, h_ref), cast_src, o_ref, cast_dst, _ = _split_refs(refs, 3, n_cast)
    acc = _dot(a_ref[...], w_ref[...])
    o_ref[...] = h_ref[...] + scale * acc
    _do_casts(cast_src, cast_dst)


def _mm_res(a, w, h, scale, casts, *, tm, tn, in_place):
    T, K = a.shape
    N = w.shape[1]
    tm = _tile(T, tm)
    tn = _tile(N, tn)
    grid = (T // tm, N // tn)
    c_in, c_out, c_shapes, c_ops = _cast_plan(casts, grid)
    est = 2 * tm * K * 2 + 2 * K * tn * 2 + 4 * tm * tn * 4 + tm * tn * 4 + _cast_bytes(casts, grid)
    outs = pl.pallas_call(
        functools.partial(_mm_res_kernel, scale=scale, n_cast=len(casts)),
        grid=grid,
        in_specs=[pl.BlockSpec((tm, K), lambda m, n: (m, 0)),
                  pl.BlockSpec((K, tn), lambda m, n: (0, n)),
                  pl.BlockSpec((tm, tn), lambda m, n: (m, n))] + c_in,
        out_specs=[pl.BlockSpec((tm, tn), lambda m, n: (m, n))] + c_out,
        out_shape=[jax.ShapeDtypeStruct((T, N), F32)] + c_shapes,
        input_output_aliases={2: 0} if in_place else {},
        compiler_params=pltpu.CompilerParams(
            dimension_semantics=("arbitrary", "arbitrary"),
            vmem_limit_bytes=_vmem_limit(est)),
        name="mm_residual",
    )(a, w, h, *c_ops)
    return outs[0], outs[1:]


def _merge_kernel(*refs, n_cast):
    (ya_ref, yb_ref, wa_ref, wb_ref, ga_ref, gb_ref), cast_src, o_ref, cast_dst, _ = _split_refs(refs, 6, n_cast)
    a = _dot(ya_ref[...], wa_ref[...])
    b = _dot(yb_ref[...], wb_ref[...])
    ga = jax.nn.sigmoid(ga_ref[...].astype(F32))
    gb = jax.nn.sigmoid(gb_ref[...].astype(F32))
    o_ref[...] = (ga * a + gb * b).astype(o_ref.dtype)
    _do_casts(cast_src, cast_dst)


def _merge(ya, yb, wa, wb, z, ga_col, gb_col, casts, *, tm, tn):
    T, KA = ya.shape
    KB = yb.shape[1]
    N = wa.shape[1]
    tm = _tile(T, tm)
    tn = _tile(N, tn)
    grid = (T // tm, N // tn)
    ga_blk, gb_blk = ga_col // tn, gb_col // tn
    assert ga_blk * tn == ga_col and gb_blk * tn == gb_col
    c_in, c_out, c_shapes, c_ops = _cast_plan(casts, grid)
    est = (2 * tm * (KA + KB) * 2 + 2 * (KA + KB) * tn * 2 + 6 * tm * tn * 2 + 4 * tm * tn * 4
           + _cast_bytes(casts, grid))
    outs = pl.pallas_call(
        functools.partial(_merge_kernel, n_cast=len(casts)),
        grid=grid,
        in_specs=[pl.BlockSpec((tm, KA), lambda m, n: (m, 0)),
                  pl.BlockSpec((tm, KB), lambda m, n: (m, 0)),
                  pl.BlockSpec((KA, tn), lambda m, n: (0, n)),
                  pl.BlockSpec((KB, tn), lambda m, n: (0, n)),
                  pl.BlockSpec((tm, tn), lambda m, n: (m, ga_blk + n)),
                  pl.BlockSpec((tm, tn), lambda m, n: (m, gb_blk + n))] + c_in,
        out_specs=[pl.BlockSpec((tm, tn), lambda m, n: (m, n))] + c_out,
        out_shape=[jax.ShapeDtypeStruct((T, N), BF16)] + c_shapes,
        compiler_params=pltpu.CompilerParams(
            dimension_semantics=("arbitrary", "arbitrary"),
            vmem_limit_bytes=_vmem_limit(est)),
        name="gated_merge",
    )(ya, yb, wa, wb, z, z, *c_ops)
    return outs[0], outs[1:]


def _rmsnorm_kernel(h_ref, g_ref, o_ref):
    x = h_ref[...]
    ms = jnp.mean(x * x, axis=-1, keepdims=True)
    o_ref[...] = x * lax.rsqrt(ms + NORM_EPS) * g_ref[...]


def _rmsnorm(h, gain, *, tr):
    T, D = h.shape
    tr = min(tr, T)
    return pl.pallas_call(
        _rmsnorm_kernel,
        grid=(T // tr,),
        in_specs=[pl.BlockSpec((tr, D), lambda i: (i, 0)),
                  pl.BlockSpec((1, D), lambda i: (0, 0))],
        out_specs=pl.BlockSpec((tr, D), lambda i: (i, 0)),
        out_shape=jax.ShapeDtypeStruct((T, D), F32),
        compiler_params=pltpu.CompilerParams(dimension_semantics=("parallel",)),
        name="final_norm",
    )(h, gain.reshape(1, D))


def _rglru_kernel(x_ref, y_ref, cw_ref, cb_ref, wa_ref, ba_ref, wx_ref, bx_ref, lam_ref,
                  o_ref, xp_sc, a_sc, b_sc, carry_sc, *, conv_width):
    tt, cb = x_ref.shape
    pad = SUBLANES

    @pl.when(pl.program_id(2) == 0)
    def _():
        xp_sc[0:pad, :] = jnp.zeros((pad, cb), F32)
        carry_sc[...] = jnp.zeros((SUBLANES, cb), F32)

    xp_sc[pad:pad + tt, :] = x_ref[...].astype(F32)
    xc = jnp.broadcast_to(cb_ref[0], (tt, cb))
    for k in range(conv_width):
        xc = xc + xp_sc[pl.ds(pad - (conv_width - 1) + k, tt), :] * cw_ref[0, k:k + 1, :]
    xp_sc[0:pad, :] = xp_sc[tt:tt + pad, :]

    xcb = xc.astype(BF16)
    r = jax.nn.sigmoid(_dot(xcb, wa_ref[0]) + ba_ref[0])
    i = jax.nn.sigmoid(_dot(xcb, wx_ref[0]) + bx_ref[0])
    log_a = (RG_C * jax.nn.log_sigmoid(lam_ref[0])) * r
    a = jnp.exp(log_a)
    a_sc[...] = a
    b_sc[...] = jnp.sqrt(-jnp.tanh(log_a) * (a * a + 1.0)) * (i * xc)

    row = lax.broadcasted_iota(jnp.int32, (SUBLANES, cb), 0)

    def group(gi, carry):
        r8 = pl.ds(pl.multiple_of(gi * SUBLANES, SUBLANES), SUBLANES)
        a = a_sc[r8, :]
        b = b_sc[r8, :]
        for d in (1, 2, 4):
            keep = row >= d
            a_prev = jnp.where(keep, pltpu.roll(a, d, 0), 1.0)
            b_prev = jnp.where(keep, pltpu.roll(b, d, 0), 0.0)
            b = a * b_prev + b
            a = a * a_prev
        h = a * carry + b
        b_sc[r8, :] = h
        return jnp.broadcast_to(h[SUBLANES - 1:SUBLANES, :], (SUBLANES, cb))

    carry_sc[...] = lax.fori_loop(0, tt // SUBLANES, group, carry_sc[...], unroll=4)
    o_ref[...] = (b_sc[...] * jax.nn.gelu(y_ref[...].astype(F32))).astype(o_ref.dtype)


def _rglru(z, cw, cb, wa, ba, wx, bx, lam, *, batch, seq, d_rg, tt):
    heads, blk = wa.shape[0], wa.shape[1]
    conv_width = cw.shape[0]
    tt = min(tt, seq)
    nt = seq // tt
    T = batch * seq
    vec = lambda v: v.reshape(heads, 1, blk)
    vspec = pl.BlockSpec((1, 1, blk), lambda b, h, t: (h, 0, 0))
    wspec = pl.BlockSpec((1, blk, blk), lambda b, h, t: (h, 0, 0))
    cwh = cw.reshape(conv_width, heads, blk).transpose(1, 0, 2)
    return pl.pallas_call(
        functools.partial(_rglru_kernel, conv_width=conv_width),
        grid=(batch, heads, nt),
        in_specs=[pl.BlockSpec((tt, blk), lambda b, h, t: (b * nt + t, h)),
                  pl.BlockSpec((tt, blk), lambda b, h, t: (b * nt + t, heads + h)),
                  pl.BlockSpec((1, conv_width, blk), lambda b, h, t: (h, 0, 0)),
                  vspec, wspec, vspec, wspec, vspec, vspec],
        out_specs=pl.BlockSpec((tt, blk), lambda b, h, t: (b * nt + t, h)),
        out_shape=jax.ShapeDtypeStruct((T, d_rg), BF16),
        scratch_shapes=[pltpu.VMEM((tt + 2 * SUBLANES, blk), F32),
                        pltpu.VMEM((tt, blk), F32),
                        pltpu.VMEM((tt, blk), F32),
                        pltpu.VMEM((SUBLANES, blk), F32)],
        compiler_params=pltpu.CompilerParams(
            dimension_semantics=("parallel", "parallel", "arbitrary")),
        name="rglru",
    )(z, z, cwh, vec(cb), wa, vec(ba), wx, vec(bx), vec(lam))


def _hg_constants(chunk):
    t = np.arange(chunk)[:, None]
    r = np.arange(chunk)[None, :]
    tril = (r <= t).astype(np.float32)
    return np.concatenate([tril, tril, tril], axis=1)


def _split3(x):
    hi = x.astype(BF16)
    r1 = x - hi.astype(F32)
    mid = r1.astype(BF16)
    lo = (r1 - mid.astype(F32)).astype(BF16)
    return hi, mid, lo


def _hg_level_operands(q, kk, fd, b2, chunk):
    dk = q.shape[1]
    zeros = lambda n: jnp.zeros((n, dk), F32)
    qes, kes, halves = [q], [kk], [0]
    m = chunk // 2
    while m >= SUBLANES:
        qparts, kparts = [], []
        for s in range(0, chunk, 2 * m):
            ref = b2[s + m - 1:s + m, :]
            qparts += [zeros(m), q[s + m:s + 2 * m] * jnp.exp2(b2[s + m:s + 2 * m] - ref)]
            kparts += [kk[s:s + m] * jnp.exp2(ref - b2[s:s + m]), zeros(m)]
        qes.append(jnp.concatenate(qparts, axis=0))
        kes.append(jnp.concatenate(kparts, axis=0))
        halves.append(m)
        m //= 2
    row = lax.broadcasted_iota(jnp.int32, (chunk, dk), 0)
    b3 = b2.reshape(chunk // SUBLANES, SUBLANES, dk)
    ref = jnp.broadcast_to(b3[:, 3:4, :], b3.shape).reshape(chunk, dk)
    w = jnp.exp2(-jnp.abs(b2 - ref))
    upper = (row & 4) != 0
    qes.append(jnp.where(upper, q * w, 0.0))
    kes.append(jnp.where(upper, 0.0, kk * w))
    halves.append(4)
    f_prev = pltpu.roll(fd, 1, 0)
    f_next = pltpu.roll(fd, chunk - 1, 0)
    r4 = row & 3
    qes.append(jnp.where(r4 >= 2, q * jnp.where(r4 == 3, fd * f_prev, fd), 0.0))
    kes.append(jnp.where(r4 >= 2, 0.0, kk * jnp.where(r4 == 0, f_next, 1.0)))
    halves.append(2)
    odd = (row & 1) != 0
    qes.append(jnp.where(odd, q * fd, 0.0))
    kes.append(jnp.where(odd, 0.0, kk))
    halves.append(1)
    return qes, kes, halves


def _hgrn2_kernel(q_ref, f_ref, i_ref, g_ref, lbc_ref, gain_ref, mc_ref, o_ref, st_sc, *, chunk, sub):
    tt, dk = q_ref.shape
    nch = sub // chunk

    @pl.when(pl.program_id(2) == 0)
    def _():
        st_sc[...] = jnp.zeros(st_sc.shape, F32)

    log_lb = lbc_ref[0, 0:1, :]
    log_1m_lb = lbc_ref[0, 1:2, :]
    one_m_lb = lbc_ref[0, 2:3, :]
    lb = lbc_ref[0, 3:4, :]
    gain = gain_ref[0]

    def body(j, carry):
        rows = pl.ds(pl.multiple_of(j * sub, sub), sub)
        zf = f_ref[rows, :].astype(F32)
        e = jnp.exp2(jnp.abs(zf) * (-LOG2E))
        den = 1.0 + e
        inv = 1.0 / den
        pos = zf >= 0.0
        log_sig = jnp.minimum(zf, 0.0) - jnp.log(den)
        einv = e * inv
        sig_pos = jnp.where(pos, inv, einv)
        sig_neg = jnp.where(pos, einv, inv)
        x2 = log_1m_lb + log_sig
        lf2 = (jnp.maximum(log_lb, x2)
               + jnp.log(1.0 + jnp.exp2(jnp.abs(log_lb - x2) * (-LOG2E)))) * LOG2E
        fd = lb + one_m_lb * sig_pos
        kk = one_m_lb * sig_neg
        qp = q_ref[rows, :].astype(F32)
        q = qp * jax.nn.sigmoid(qp)
        v = i_ref[rows, :]

        w3 = jnp.concatenate(
            [jnp.concatenate([p[c * chunk:(c + 1) * chunk] for c in range(nch)], axis=1)
             for p in _split3(lf2)], axis=0)
        xw = _dot(mc_ref[...], w3)

        xr = (lax.broadcasted_iota(jnp.int32, (chunk, chunk), 0)
              ^ lax.broadcasted_iota(jnp.int32, (chunk, chunk), 1))
        ps, qds, upds, decs = [], [], [], []
        for c in range(nch):
            sl = slice(c * chunk, (c + 1) * chunk)
            b2 = xw[:, c * dk:(c + 1) * dk]
            qc, kc = q[sl], kk[sl]
            qes, kes, halves = _hg_level_operands(qc, kc, fd[sl], b2, chunk)
            p = jnp.where(xr == 0, _dot_dims(qes[0].astype(BF16), kes[0].astype(BF16), NT_DIMS), 0.0)
            for lv in range(1, len(halves)):
                s = _dot_dims(qes[lv].astype(BF16), kes[lv].astype(BF16), NT_DIMS)
                p = jnp.where((xr >> int(math.log2(halves[lv]))) == 1, s, p)
            ps.append(p.astype(BF16))
            qds.append((qc * jnp.exp2(b2)).astype(BF16))
            b_last = b2[chunk - 1:chunk, :]
            k_dec = (kc * jnp.exp2(b_last - b2)).astype(BF16)
            upds.append(_dot_dims(v[sl], k_dec, TN_DIMS))
            decs.append(jnp.exp2(b_last))
        st = st_sc[...]
        outs = []
        for c in range(nch):
            sl = slice(c * chunk, (c + 1) * chunk)
            outs.append(_dot(ps[c], v[sl]) + _dot_dims(qds[c], st.astype(BF16), NT_DIMS))
            st = st * decs[c] + upds[c]
        st_sc[...] = st

        o = jnp.concatenate(outs, axis=0)
        o = o * lax.rsqrt(jnp.mean(o * o, axis=-1, keepdims=True) + NORM_EPS) * gain
        gp = g_ref[rows, :].astype(F32)
        o_ref[rows, :] = (o * (gp * jax.nn.sigmoid(gp))).astype(o_ref.dtype)
        return carry

    lax.fori_loop(0, tt // sub, body, 0)


def _hgrn2(z, lbc, gain, *, batch, seq, d_hg, col0, tt, sub, chunk):
    dk = HG_HEAD_DIM
    heads = d_hg // dk
    tt = min(tt, seq)
    sub = min(sub, tt)
    chunk = min(chunk, sub)
    assert chunk >= 2 * SUBLANES and tt % sub == 0 and sub % chunk == 0
    nt = seq // tt
    T = batch * seq
    mc = jnp.asarray(_hg_constants(chunk), BF16)
    base = col0 // dk

    def zspec(j):
        return pl.BlockSpec((tt, dk), lambda b, h, t: (b * nt + t, base + j * heads + h))

    return pl.pallas_call(
        functools.partial(_hgrn2_kernel, chunk=chunk, sub=sub),
        grid=(batch, heads, nt),
        in_specs=[zspec(0), zspec(1), zspec(2), zspec(3),
                  pl.BlockSpec((1, 4, dk), lambda b, h, t: (h, 0, 0)),
                  pl.BlockSpec((1, 1, dk), lambda b, h, t: (h, 0, 0)),
                  pl.BlockSpec(mc.shape, lambda b, h, t: (0, 0))],
        out_specs=pl.BlockSpec((tt, dk), lambda b, h, t: (b * nt + t, h)),
        out_shape=jax.ShapeDtypeStruct((T, d_hg), BF16),
        scratch_shapes=[pltpu.VMEM((dk, dk), F32)],
        compiler_params=pltpu.CompilerParams(
            dimension_semantics=("parallel", "parallel", "arbitrary")),
        name="hgrn2",
    )(z, z, z, z, lbc, gain.reshape(heads, 1, dk), mc)


def _lb_kernel(p_ref, o_ref):
    depth = p_ref.shape[0]
    p = p_ref[...]
    mx = jnp.max(p, axis=0, keepdims=True)
    ex = jnp.exp(p - mx)
    sm = ex / jnp.sum(ex, axis=0, keepdims=True)
    c0 = sm[0:1, :]
    c = c0
    for l in range(depth):
        if l > 0:
            c = c + sm[l:l + 1, :]
        lb = c - c0
        o_ref[l, 0:1, :] = jnp.log(lb)
        o_ref[l, 1:2, :] = jnp.log1p(-lb)
        o_ref[l, 2:3, :] = 1.0 - lb
        o_ref[l, 3:4, :] = lb


def _lb_constants(lb_param):
    depth, d_hg = lb_param.shape
    return pl.pallas_call(
        _lb_kernel,
        out_shape=jax.ShapeDtypeStruct((depth, 4, d_hg), F32),
        name="hgrn2_lower_bounds",
    )(lb_param)


def kernel(x, ffn1_norm, ffn1_w_gate, ffn1_w_up, ffn1_w_down, mix_norm, w_in, conv_w, conv_b, rg_wa, rg_ba, rg_wx, rg_bx, rg_lambda, hg_lower_bounds, hg_out_norm, w_out_a, w_out_b, w_out, ffn2_norm, ffn2_w_gate, ffn2_w_up, ffn2_w_down, final_norm):
    batch, seq, d_model = x.shape
    depth = w_in.shape[0]
    d_rg = rg_lambda.shape[1]
    d_hg = hg_lower_bounds.shape[1]
    heads_hg = d_hg // HG_HEAD_DIM
    T = batch * seq
    tm, tn = ROW_TILE, COL_TILE

    lbc = _lb_constants(hg_lower_bounds)
    lbc = lbc.reshape(depth, 4, heads_hg, HG_HEAD_DIM).transpose(0, 2, 1, 3)

    bf = lambda w: w.astype(BF16)
    names = ("wg1", "wu1", "wd1", "w_in", "wa", "wb", "w_out", "wg2", "wu2", "wd2")
    stacks = dict(zip(names, (ffn1_w_gate, ffn1_w_up, ffn1_w_down, w_in, w_out_a, w_out_b, w_out,
                              ffn2_w_gate, ffn2_w_up, ffn2_w_down)))
    wt = {k: bf(v[0]) for k, v in stacks.items()}
    rg_wa16, rg_wx16 = bf(rg_wa), bf(rg_wx)

    h = x.reshape(T, d_model)
    for l in range(depth):
        nxt = {}

        def casts(*keys):
            return [(stacks[k], l + 1) for k in keys] if l + 1 < depth else []

        def keep(keys, outs):
            nxt.update(zip(keys, outs) if l + 1 < depth else ())

        act, c = _norm_mm(h, ffn1_norm[l], [wt["wg1"], wt["wu1"]], casts("wg1", "wu1"), tm=tm // 2, tn=tn)
        keep(("wg1", "wu1"), c)
        h, c = _mm_res(act, wt["wd1"], h, FFN_RES_WEIGHT, casts("wd1"), tm=tm, tn=tn, in_place=l > 0)
        keep(("wd1",), c)

        z, c = _norm_mm(h, mix_norm[l], [wt["w_in"]], casts("w_in"), tm=tm // 2, tn=2 * tn)
        keep(("w_in",), c)
        y_a = _rglru(z, conv_w[l], conv_b[l], rg_wa16[l], rg_ba[l], rg_wx16[l], rg_bx[l],
                     rg_lambda[l], batch=batch, seq=seq, d_rg=d_rg, tt=SEQ_TILE)
        y_b = _hgrn2(z, lbc[l], hg_out_norm[l], batch=batch, seq=seq, d_hg=d_hg,
                     col0=2 * d_rg, tt=HG_SEQ_TILE, sub=HG_SUB, chunk=HG_CHUNK)
        merged, c = _merge(y_a, y_b, wt["wa"], wt["wb"], z,
                           2 * d_rg + 4 * d_hg, 2 * d_rg + 4 * d_hg + d_model, casts("wa", "wb"), tm=tm, tn=tn)
        keep(("wa", "wb"), c)
        h, c = _mm_res(merged, wt["w_out"], h, 1.0, casts("w_out"), tm=tm, tn=tn, in_place=True)
        keep(("w_out",), c)

        act, c = _norm_mm(h, ffn2_norm[l], [wt["wg2"], wt["wu2"]], casts("wg2", "wu2"), tm=tm // 2, tn=tn)
        keep(("wg2", "wu2"), c)
        h, c = _mm_res(act, wt["wd2"], h, FFN_RES_WEIGHT, casts("wd2"), tm=tm, tn=tn, in_place=True)
        keep(("wd2",), c)
        wt = nxt
    out = _rmsnorm(h, final_norm, tr=256)
    return out.reshape(batch, seq, d_model)
```

```python
import functools
import math

import numpy as np
import jax
import jax.numpy as jnp
from jax import lax
from jax.experimental import pallas as pl
from jax.experimental.pallas import tpu as pltpu

F32 = jnp.float32
BF16 = jnp.bfloat16

NORM_EPS = 1e-6
FFN_RES_WEIGHT = 0.5
RG_C = 8.0
HG_HEAD_DIM = 128

V7X_VMEM_BYTES = 64 * 1024 * 1024
SUBLANES = 8
LANES = 128
BF16_ROWS = 16

ROW_TILE = 1024
COL_TILE = 512
NORM_ROWS = 128
SEQ_TILE = 512
HG_SEQ_TILE = 1024
HG_SUB = 512
HG_CHUNK = 64
MIX_K_PIECES = 8

LOG2E = 1.4426950408889634
NT_DIMS = (((1,), (1,)), ((), ()))
TN_DIMS = (((0,), (0,)), ((), ()))


def _vmem_limit(block_bytes):
    return int(min(V7X_VMEM_BYTES - 4 * 1024 * 1024, block_bytes * 1.3 + 8 * 1024 * 1024))


def _tile(n, pref):
    t = min(pref, n)
    while n % t:
        t //= 2
    return t


def _dot(a, b):
    return jnp.dot(a, b, preferred_element_type=F32)


def _dot_dims(a, b, dims):
    return lax.dot_general(a, b, dims, preferred_element_type=F32)


def _cast_rows(rows, steps):
    pr = BF16_ROWS
    while rows % pr or rows // pr > steps:
        pr += BF16_ROWS
    return pr


def _cast_plan(casts, grid):
    steps = grid[0] * grid[1]
    in_specs, out_specs, out_shapes, operands = [], [], [], []
    for stack, layer in casts:
        _, R, C = stack.shape
        pr = _cast_rows(R, steps)
        pieces = R // pr

        def piece(m, n, pieces=pieces):
            return jnp.minimum(m * grid[1] + n, pieces - 1)

        in_specs.append(pl.BlockSpec((None, pr, C), lambda m, n, layer=layer, piece=piece: (layer, piece(m, n), 0)))
        out_specs.append(pl.BlockSpec((pr, C), lambda m, n, piece=piece: (piece(m, n), 0)))
        out_shapes.append(jax.ShapeDtypeStruct((R, C), BF16))
        operands.append(stack)
    return in_specs, out_specs, out_shapes, operands


def _cast_bytes(casts, grid):
    steps = grid[0] * grid[1]
    total = 0
    for stack, _ in casts:
        _, R, C = stack.shape
        total += 2 * _cast_rows(R, steps) * C * (4 + 2)
    return total


def _do_casts(src_refs, dst_refs, perms=None):
    for k, (s, d) in enumerate(zip(src_refs, dst_refs)):
        perm = perms[k] if perms else None
        if perm is None:
            d[...] = s[...].astype(BF16)
        else:
            d[...] = jnp.concatenate([s[:, b * LANES:(b + 1) * LANES] for b in perm], axis=1).astype(BF16)


def _split_refs(refs, n_in, n_cast):
    ins = refs[:n_in]
    cast_src = refs[n_in:n_in + n_cast]
    out = refs[n_in + n_cast]
    cast_dst = refs[n_in + n_cast + 1:n_in + 2 * n_cast + 1]
    scratch = refs[n_in + 2 * n_cast + 1:]
    return ins, cast_src, out, cast_dst, scratch


def _norm_mm_kernel(*refs, n_w, n_cast):
    (h_ref, g_ref, *w_refs), cast_src, o_ref, cast_dst, (u_sc,) = _split_refs(refs, 2 + n_w, n_cast)

    @pl.when(pl.program_id(1) == 0)
    def _():
        rows = h_ref.shape[0]
        step = min(NORM_ROWS, rows)

        def body(i, c):
            r = pl.ds(pl.multiple_of(i * step, step), step)
            x = h_ref[r, :]
            ms = jnp.mean(x * x, axis=-1, keepdims=True)
            u_sc[r, :] = (x * lax.rsqrt(ms + NORM_EPS) * g_ref[...]).astype(BF16)
            return c

        lax.fori_loop(0, rows // step, body, 0)

    u = u_sc[...]
    if n_w == 2:
        g = _dot(u, w_refs[0][...])
        up = _dot(u, w_refs[1][...])
        o_ref[...] = (g * jax.nn.sigmoid(g) * up).astype(o_ref.dtype)
    else:
        o_ref[...] = _dot(u, w_refs[0][...]).astype(o_ref.dtype)
    _do_casts(cast_src, cast_dst)


def _norm_mm(h, gain, ws, casts, *, tm, tn):
    T, D = h.shape
    N = ws[0].shape[1]
    tm = _tile(T, tm)
    tn = _tile(N, tn)
    grid = (T // tm, N // tn)
    w_spec = pl.BlockSpec((D, tn), lambda m, n: (0, n))
    c_in, c_out, c_shapes, c_ops = _cast_plan(casts, grid)
    est = (2 * tm * D * 4 + tm * D * 2 + len(ws) * 2 * D * tn * 2 + 2 * tm * tn * 2
           + (1 + len(ws)) * tm * tn * 4 + _cast_bytes(casts, grid))
    outs = pl.pallas_call(
        functools.partial(_norm_mm_kernel, n_w=len(ws), n_cast=len(casts)),
        grid=grid,
        in_specs=[pl.BlockSpec((tm, D), lambda m, n: (m, 0)),
                  pl.BlockSpec((1, D), lambda m, n: (0, 0))] + [w_spec] * len(ws) + c_in,
        out_specs=[pl.BlockSpec((tm, tn), lambda m, n: (m, n))] + c_out,
        out_shape=[jax.ShapeDtypeStruct((T, N), BF16)] + c_shapes,
        scratch_shapes=[pltpu.VMEM((tm, D), BF16)],
        compiler_params=pltpu.CompilerParams(
            dimension_semantics=("arbitrary", "arbitrary"),
            vmem_limit_bytes=_vmem_limit(est)),
        name="norm_glu" if len(ws) == 2 else "norm_inproj",
    )(h, gain.reshape(1, D), *ws, *c_ops)
    return outs[0], outs[1:]


def _mm_res_kernel(*refs, scale, n_cast):
    (a_ref, w_ref, h_ref), cast_src, o_ref, cast_dst, _ = _split_refs(refs, 3, n_cast)
    acc = _dot(a_ref[...], w_ref[...])
    o_ref[...] = h_ref[...] + scale * acc
    _do_casts(cast_src, cast_dst)


def _mm_res(a, w, h, scale, casts, *, tm, tn, in_place):
    T, K = a.shape
    N = w.shape[1]
    tm = _tile(T, tm)
    tn = _tile(N, tn)
    grid = (T // tm, N // tn)
    c_in, c_out, c_shapes, c_ops = _cast_plan(casts, grid)
    est = 2 * tm * K * 2 + 2 * K * tn * 2 + 4 * tm * tn * 4 + tm * tn * 4 + _cast_bytes(casts, grid)
    outs = pl.pallas_call(
        functools.partial(_mm_res_kernel, scale=scale, n_cast=len(casts)),
        grid=grid,
        in_specs=[pl.BlockSpec((tm, K), lambda m, n: (m, 0)),
                  pl.BlockSpec((K, tn), lambda m, n: (0, n)),
                  pl.BlockSpec((tm, tn), lambda m, n: (m, n))] + c_in,
        out_specs=[pl.BlockSpec((tm, tn), lambda m, n: (m, n))] + c_out,
        out_shape=[jax.ShapeDtypeStruct((T, N), F32)] + c_shapes,
        input_output_aliases={2: 0} if in_place else {},
        compiler_params=pltpu.CompilerParams(
            dimension_semantics=("arbitrary", "arbitrary"),
            vmem_limit_bytes=_vmem_limit(est)),
        name="mm_residual",
    )(a, w, h, *c_ops)
    return outs[0], outs[1:]


def _merge_kernel(*refs, n_cast):
    (ya_ref, yb_ref, wa_ref, wb_ref, ga_ref, gb_ref), cast_src, o_ref, cast_dst, _ = _split_refs(refs, 6, n_cast)
    a = _dot(ya_ref[...], wa_ref[...])
    b = _dot(yb_ref[...], wb_ref[...])
    ga = jax.nn.sigmoid(ga_ref[...].astype(F32))
    gb = jax.nn.sigmoid(gb_ref[...].astype(F32))
    o_ref[...] = (ga * a + gb * b).astype(o_ref.dtype)
    _do_casts(cast_src, cast_dst)


def _merge(ya, yb, wa, wb, z, ga_col, gb_col, casts, *, tm, tn):
    T, KA = ya.shape
    KB = yb.shape[1]
    N = wa.shape[1]
    tm = _tile(T, tm)
    tn = _tile(N, tn)
    grid = (T // tm, N // tn)
    ga_blk, gb_blk = ga_col // tn, gb_col // tn
    assert ga_blk * tn == ga_col and gb_blk * tn == gb_col
    c_in, c_out, c_shapes, c_ops = _cast_plan(casts, grid)
    est = (2 * tm * (KA + KB) * 2 + 2 * (KA + KB) * tn * 2 + 6 * tm * tn * 2 + 4 * tm * tn * 4
           + _cast_bytes(casts, grid))
    outs = pl.pallas_call(
        functools.partial(_merge_kernel, n_cast=len(casts)),
        grid=grid,
        in_specs=[pl.BlockSpec((tm, KA), lambda m, n: (m, 0)),
                  pl.BlockSpec((tm, KB), lambda m, n: (m, 0)),
                  pl.BlockSpec((KA, tn), lambda m, n: (0, n)),
                  pl.BlockSpec((KB, tn), lambda m, n: (0, n)),
                  pl.BlockSpec((tm, tn), lambda m, n: (m, ga_blk + n)),
                  pl.BlockSpec((tm, tn), lambda m, n: (m, gb_blk + n))] + c_in,
        out_specs=[pl.BlockSpec((tm, tn), lambda m, n: (m, n))] + c_out,
        out_shape=[jax.ShapeDtypeStruct((T, N), BF16)] + c_shapes,
        compiler_params=pltpu.CompilerParams(
            dimension_semantics=("arbitrary", "arbitrary"),
            vmem_limit_bytes=_vmem_limit(est)),
        name="gated_merge",
    )(ya, yb, wa, wb, z, z, *c_ops)
    return outs[0], outs[1:]


def _rmsnorm_kernel(h_ref, g_ref, o_ref):
    x = h_ref[...]
    ms = jnp.mean(x * x, axis=-1, keepdims=True)
    o_ref[...] = x * lax.rsqrt(ms + NORM_EPS) * g_ref[...]


def _rmsnorm(h, gain, *, tr):
    T, D = h.shape
    tr = min(tr, T)
    return pl.pallas_call(
        _rmsnorm_kernel,
        grid=(T // tr,),
        in_specs=[pl.BlockSpec((tr, D), lambda i: (i, 0)),
                  pl.BlockSpec((1, D), lambda i: (0, 0))],
        out_specs=pl.BlockSpec((tr, D), lambda i: (i, 0)),
        out_shape=jax.ShapeDtypeStruct((T, D), F32),
        compiler_params=pltpu.CompilerParams(dimension_semantics=("parallel",)),
        name="final_norm",
    )(h, gain.reshape(1, D))


def _rglru_kernel(x_ref, y_ref, cw_ref, cb_ref, wa_ref, ba_ref, wx_ref, bx_ref, lam_ref,
                  o_ref, xp_sc, a_sc, b_sc, carry_sc, *, conv_width):
    tt, cb = x_ref.shape
    pad = SUBLANES

    @pl.when(pl.program_id(2) == 0)
    def _():
        xp_sc[0:pad, :] = jnp.zeros((pad, cb), F32)
        carry_sc[...] = jnp.zeros((SUBLANES, cb), F32)

    xp_sc[pad:pad + tt, :] = x_ref[...].astype(F32)
    xc = jnp.broadcast_to(cb_ref[0], (tt, cb))
    for k in range(conv_width):
        xc = xc + xp_sc[pl.ds(pad - (conv_width - 1) + k, tt), :] * cw_ref[0, k:k + 1, :]
    xp_sc[0:pad, :] = xp_sc[tt:tt + pad, :]

    xcb = xc.astype(BF16)
    r = jax.nn.sigmoid(_dot(xcb, wa_ref[0]) + ba_ref[0])
    i = jax.nn.sigmoid(_dot(xcb, wx_ref[0]) + bx_ref[0])
    log_a = (RG_C * jax.nn.log_sigmoid(lam_ref[0])) * r
    a = jnp.exp(log_a)
    a_sc[...] = a
    b_sc[...] = jnp.sqrt(-jnp.tanh(log_a) * (a * a + 1.0)) * (i * xc)

    row = lax.broadcasted_iota(jnp.int32, (SUBLANES, cb), 0)

    def group(gi, carry):
        r8 = pl.ds(pl.multiple_of(gi * SUBLANES, SUBLANES), SUBLANES)
        a = a_sc[r8, :]
        b = b_sc[r8, :]
        for d in (1, 2, 4):
            keep = row >= d
            a_prev = jnp.where(keep, pltpu.roll(a, d, 0), 1.0)
            b_prev = jnp.where(keep, pltpu.roll(b, d, 0), 0.0)
            b = a * b_prev + b
            a = a * a_prev
        h = a * carry + b
        b_sc[r8, :] = h
        return jnp.broadcast_to(h[SUBLANES - 1:SUBLANES, :], (SUBLANES, cb))

    carry_sc[...] = lax.fori_loop(0, tt // SUBLANES, group, carry_sc[...], unroll=4)
    o_ref[...] = (b_sc[...] * jax.nn.gelu(y_ref[...].astype(F32))).astype(o_ref.dtype)


def _rglru(z, cw, cb, wa, ba, wx, bx, lam, *, batch, seq, d_rg, tt):
    heads, blk = wa.shape[0], wa.shape[1]
    conv_width = cw.shape[0]
    tt = min(tt, seq)
    nt = seq // tt
    T = batch * seq
    vec = lambda v: v.reshape(heads, 1, blk)
    vspec = pl.BlockSpec((1, 1, blk), lambda b, h, t: (h, 0, 0))
    wspec = pl.BlockSpec((1, blk, blk), lambda b, h, t: (h, 0, 0))
    cwh = cw.reshape(conv_width, heads, blk).transpose(1, 0, 2)
    return pl.pallas_call(
        functools.partial(_rglru_kernel, conv_width=conv_width),
        grid=(batch, heads, nt),
        in_specs=[pl.BlockSpec((tt, blk), lambda b, h, t: (b * nt + t, h)),
                  pl.BlockSpec((tt, blk), lambda b, h, t: (b * nt + t, heads + h)),
                  pl.BlockSpec((1, conv_width, blk), lambda b, h, t: (h, 0, 0)),
                  vspec, wspec, vspec, wspec, vspec, vspec],
        out_specs=pl.BlockSpec((tt, blk), lambda b, h, t: (b * nt + t, h)),
        out_shape=jax.ShapeDtypeStruct((T, d_rg), BF16),
        scratch_shapes=[pltpu.VMEM((tt + 2 * SUBLANES, blk), F32),
                        pltpu.VMEM((tt, blk), F32),
                        pltpu.VMEM((tt, blk), F32),
                        pltpu.VMEM((SUBLANES, blk), F32)],
        compiler_params=pltpu.CompilerParams(
            dimension_semantics=("parallel", "parallel", "arbitrary")),
        name="rglru",
    )(z, z, cwh, vec(cb), wa, vec(ba), wx, vec(bx), vec(lam))


def _hg_constants(chunk):
    t = np.arange(chunk)[:, None]
    r = np.arange(chunk)[None, :]
    tril = (r <= t).astype(np.float32)
    return np.concatenate([tril, tril, tril], axis=1)


def _split3(x):
    hi = x.astype(BF16)
    r1 = x - hi.astype(F32)
    mid = r1.astype(BF16)
    lo = (r1 - mid.astype(F32)).astype(BF16)
    return hi, mid, lo


def _hg_level_operands(q, kk, fd, b2, chunk):
    dk = q.shape[1]
    zeros = lambda n: jnp.zeros((n, dk), F32)
    qes, kes, halves = [q], [kk], [0]
    m = chunk // 2
    while m >= SUBLANES:
        qparts, kparts = [], []
        for s in range(0, chunk, 2 * m):
            ref = b2[s + m - 1:s + m, :]
            qparts += [zeros(m), q[s + m:s + 2 * m] * jnp.exp2(b2[s + m:s + 2 * m] - ref)]
            kparts += [kk[s:s + m] * jnp.exp2(ref - b2[s:s + m]), zeros(m)]
        qes.append(jnp.concatenate(qparts, axis=0))
        kes.append(jnp.concatenate(kparts, axis=0))
        halves.append(m)
        m //= 2
    row = lax.broadcasted_iota(jnp.int32, (chunk, dk), 0)
    b3 = b2.reshape(chunk // SUBLANES, SUBLANES, dk)
    ref = jnp.broadcast_to(b3[:, 3:4, :], b3.shape).reshape(chunk, dk)
    w = jnp.exp2(-jnp.abs(b2 - ref))
    upper = (row & 4) != 0
    qes.append(jnp.where(upper, q * w, 0.0))
    kes.append(jnp.where(upper, 0.0, kk * w))
    halves.append(4)
    f_prev = pltpu.roll(fd, 1, 0)
    f_next = pltpu.roll(fd, chunk - 1, 0)
    r4 = row & 3
    qes.append(jnp.where(r4 >= 2, q * jnp.where(r4 == 3, fd * f_prev, fd), 0.0))
    kes.append(jnp.where(r4 >= 2, 0.0, kk * jnp.where(r4 == 0, f_next, 1.0)))
    halves.append(2)
    odd = (row & 1) != 0
    qes.append(jnp.where(odd, q * fd, 0.0))
    kes.append(jnp.where(odd, 0.0, kk))
    halves.append(1)
    return qes, kes, halves


def _hgrn2_kernel(q_ref, f_ref, i_ref, g_ref, lbc_ref, gain_ref, mc_ref, o_ref, st_sc, *, chunk, sub):
    tt, dk = q_ref.shape
    nch = sub // chunk

    @pl.when(pl.program_id(2) == 0)
    def _():
        st_sc[...] = jnp.zeros(st_sc.shape, F32)

    log_lb = lbc_ref[0, 0:1, :]
    log_1m_lb = lbc_ref[0, 1:2, :]
    one_m_lb = lbc_ref[0, 2:3, :]
    lb = lbc_ref[0, 3:4, :]
    gain = gain_ref[0]

    def body(j, carry):
        rows = pl.ds(pl.multiple_of(j * sub, sub), sub)
        zf = f_ref[rows, :].astype(F32)
        e = jnp.exp2(jnp.abs(zf) * (-LOG2E))
        den = 1.0 + e
        inv = 1.0 / den
        pos = zf >= 0.0
        log_sig = jnp.minimum(zf, 0.0) - jnp.log(den)
        einv = e * inv
        sig_pos = jnp.where(pos, inv, einv)
        sig_neg = jnp.where(pos, einv, inv)
        x2 = log_1m_lb + log_sig
        lf2 = (jnp.maximum(log_lb, x2)
               + jnp.log(1.0 + jnp.exp2(jnp.abs(log_lb - x2) * (-LOG2E)))) * LOG2E
        fd = lb + one_m_lb * sig_pos
        kk = one_m_lb * sig_neg
        qp = q_ref[rows, :].astype(F32)
        q = qp * jax.nn.sigmoid(qp)
        v = i_ref[rows, :]

        w3 = jnp.concatenate(
            [jnp.concatenate([p[c * chunk:(c + 1) * chunk] for c in range(nch)], axis=1)
             for p in _split3(lf2)], axis=0)
        xw = _dot(mc_ref[...], w3)

        xr = (lax.broadcasted_iota(jnp.int32, (chunk, chunk), 0)
              ^ lax.broadcasted_iota(jnp.int32, (chunk, chunk), 1))
        ps, qds, upds, decs = [], [], [], []
        for c in range(nch):
            sl = slice(c * chunk, (c + 1) * chunk)
            b2 = xw[:, c * dk:(c + 1) * dk]
            qc, kc = q[sl], kk[sl]
            qes, kes, halves = _hg_level_operands(qc, kc, fd[sl], b2, chunk)
            p = jnp.where(xr == 0, _dot_dims(qes[0].astype(BF16), kes[0].astype(BF16), NT_DIMS), 0.0)
            for lv in range(1, len(halves)):
                s = _dot_dims(qes[lv].astype(BF16), kes[lv].astype(BF16), NT_DIMS)
                p = jnp.where((xr >> int(math.log2(halves[lv]))) == 1, s, p)
            ps.append(p.astype(BF16))
            qds.append((qc * jnp.exp2(b2)).astype(BF16))
            b_last = b2[chunk - 1:chunk, :]
            k_dec = (kc * jnp.exp2(b_last - b2)).astype(BF16)
            upds.append(_dot_dims(v[sl], k_dec, TN_DIMS))
            decs.append(jnp.exp2(b_last))
        st = st_sc[...]
        outs = []
        for c in range(nch):
            sl = slice(c * chunk, (c + 1) * chunk)
            outs.append(_dot(ps[c], v[sl]) + _dot_dims(qds[c], st.astype(BF16), NT_DIMS))
            st = st * decs[c] + upds[c]
        st_sc[...] = st

        o = jnp.concatenate(outs, axis=0)
        o = o * lax.rsqrt(jnp.mean(o * o, axis=-1, keepdims=True) + NORM_EPS) * gain
        gp = g_ref[rows, :].astype(F32)
        o_ref[rows, :] = (o * (gp * jax.nn.sigmoid(gp))).astype(o_ref.dtype)
        return carry

    lax.fori_loop(0, tt // sub, body, 0)


def _hgrn2(z, lbc, gain, *, batch, seq, d_hg, col0, tt, sub, chunk):
    dk = HG_HEAD_DIM
    heads = d_hg // dk
    tt = min(tt, seq)
    sub = min(sub, tt)
    chunk = min(chunk, sub)
    assert chunk >= 2 * SUBLANES and tt % sub == 0 and sub % chunk == 0
    nt = seq // tt
    T = batch * seq
    mc = jnp.asarray(_hg_constants(chunk), BF16)
    base = col0 // dk

    def zspec(j):
        return pl.BlockSpec((tt, dk), lambda b, h, t: (b * nt + t, base + j * heads + h))

    return pl.pallas_call(
        functools.partial(_hgrn2_kernel, chunk=chunk, sub=sub),
        grid=(batch, heads, nt),
        in_specs=[zspec(0), zspec(1), zspec(2), zspec(3),
                  pl.BlockSpec((1, 4, dk), lambda b, h, t: (h, 0, 0)),
                  pl.BlockSpec((1, 1, dk), lambda b, h, t: (h, 0, 0)),
                  pl.BlockSpec(mc.shape, lambda b, h, t: (0, 0))],
        out_specs=pl.BlockSpec((tt, dk), lambda b, h, t: (b * nt + t, h)),
        out_shape=jax.ShapeDtypeStruct((T, d_hg), BF16),
        scratch_shapes=[pltpu.VMEM((dk, dk), F32)],
        compiler_params=pltpu.CompilerParams(
            dimension_semantics=("parallel", "parallel", "arbitrary")),
        name="hgrn2",
    )(z, z, z, z, lbc, gain.reshape(heads, 1, dk), mc)


def _lb_kernel(p_ref, o_ref):
    depth = p_ref.shape[0]
    p = p_ref[...]
    mx = jnp.max(p, axis=0, keepdims=True)
    ex = jnp.exp(p - mx)
    sm = ex / jnp.sum(ex, axis=0, keepdims=True)
    c0 = sm[0:1, :]
    c = c0
    for l in range(depth):
        if l > 0:
            c = c + sm[l:l + 1, :]
        lb = c - c0
        o_ref[l, 0:1, :] = jnp.log(lb)
        o_ref[l, 1:2, :] = jnp.log1p(-lb)
        o_ref[l, 2:3, :] = 1.0 - lb
        o_ref[l, 3:4, :] = lb


def _lb_constants(lb_param):
    depth, d_hg = lb_param.shape
    return pl.pallas_call(
        _lb_kernel,
        out_shape=jax.ShapeDtypeStruct((depth, 4, d_hg), F32),
        name="hgrn2_lower_bounds",
    )(lb_param)


def _mix_column_blocks(d_rg, d_hg, d_model, rg_blk, dk):
    nb = lambda c: c // LANES
    y0, q0 = nb(d_rg), nb(2 * d_rg)
    f0, i0, g0 = q0 + nb(d_hg), q0 + 2 * nb(d_hg), q0 + 3 * nb(d_hg)
    gates0 = q0 + 4 * nb(d_hg)
    order = []
    per = nb(rg_blk)
    for j in range(d_rg // rg_blk):
        order += [j * per + t for t in range(per)] + [y0 + j * per + t for t in range(per)]
    per = nb(dk)
    for j in range(d_hg // dk):
        for base in (q0, f0, i0, g0):
            order += [base + j * per + t for t in range(per)]
    order += list(range(gates0, gates0 + 2 * nb(d_model)))
    return tuple(order)


def _rg_task(zt_ref, head, first, prm, tail_sc, carry_sc, ya_ref, big, conv_width):
    cw_ref, cb_ref, wa_ref, ba_ref, wx_ref, bx_ref, lam_ref = prm
    tm = zt_ref.shape[0]
    blk = wa_ref.shape[1]
    acc = big(0) + big(1)
    x = zt_ref[:, 0:blk].astype(F32)
    tail = jnp.where(first, 0.0, tail_sc[head])
    xpad = jnp.concatenate([tail, x], axis=0)
    cw = cw_ref[head]
    xc = cb_ref[head] + x * cw[conv_width - 1:conv_width, :]
    for s in range(1, conv_width):
        xc = xc + pltpu.roll(xpad, s, 0)[SUBLANES:] * cw[conv_width - 1 - s:conv_width - s, :]
    tail_sc[head] = x[tm - SUBLANES:tm]

    xcb = xc.astype(BF16)
    r = jax.nn.sigmoid(_dot(xcb, wa_ref[head]) + ba_ref[head])
    i = jax.nn.sigmoid(_dot(xcb, wx_ref[head]) + bx_ref[head])
    for k in range(2, MIX_K_PIECES):
        acc = acc + big(k)
    log_a = (RG_C * jax.nn.log_sigmoid(lam_ref[head])) * r
    a = jnp.exp(log_a)
    b = jnp.sqrt(-jnp.tanh(log_a) * (a * a + 1.0)) * (i * xc)

    row8 = lax.broadcasted_iota(jnp.int32, (tm, blk), 0) & (SUBLANES - 1)
    for d in (1, 2, 4):
        keep = row8 >= d
        a_prev = jnp.where(keep, pltpu.roll(a, d, 0), 1.0)
        b_prev = jnp.where(keep, pltpu.roll(b, d, 0), 0.0)
        b = a * b_prev + b
        a = a * a_prev
    carry = jnp.where(first, 0.0, carry_sc[head])
    hs = []
    for gi in range(tm // SUBLANES):
        g8 = slice(gi * SUBLANES, (gi + 1) * SUBLANES)
        hg = a[g8] * carry + b[g8]
        hs.append(hg)
        carry = jnp.broadcast_to(hg[SUBLANES - 1:SUBLANES, :], (SUBLANES, blk))
    carry_sc[head] = carry
    hseq = jnp.concatenate(hs, axis=0)
    y = zt_ref[:, blk:2 * blk].astype(F32)
    ya_ref[:, pl.ds(pl.multiple_of(head * blk, blk), blk)] = (hseq * jax.nn.gelu(y)).astype(ya_ref.dtype)
    return acc


def _hg_task(zt_ref, head, first, prm, st_sc, yb_ref, big, chunk):
    lbc_ref, gain_ref, mc_ref = prm
    tm = zt_ref.shape[0]
    dk = HG_HEAD_DIM
    nch = tm // chunk
    lbc = lbc_ref[head]
    log_lb, log_1m_lb, one_m_lb, lb = lbc[0:1], lbc[1:2], lbc[2:3], lbc[3:4]

    acc = big(0) + big(1)
    zf = zt_ref[:, dk:2 * dk].astype(F32)
    e = jnp.exp2(jnp.abs(zf) * (-LOG2E))
    den = 1.0 + e
    inv = 1.0 / den
    pos = zf >= 0.0
    log_sig = jnp.minimum(zf, 0.0) - jnp.log(den)
    einv = e * inv
    sig_pos = jnp.where(pos, inv, einv)
    sig_neg = jnp.where(pos, einv, inv)
    x2 = log_1m_lb + log_sig
    lf2 = (jnp.maximum(log_lb, x2)
           + jnp.log(1.0 + jnp.exp2(jnp.abs(log_lb - x2) * (-LOG2E)))) * LOG2E
    fd = lb + one_m_lb * sig_pos
    kk = one_m_lb * sig_neg
    qp = zt_ref[:, 0:dk].astype(F32)
    q = qp * jax.nn.sigmoid(qp)
    v = zt_ref[:, 2 * dk:3 * dk]

    w3 = jnp.concatenate(
        [jnp.concatenate([p[c * chunk:(c + 1) * chunk] for c in range(nch)], axis=1)
         for p in _split3(lf2)], axis=0)
    xw = _dot(mc_ref[...], w3)
    acc = acc + big(2) + big(3)

    xr = (lax.broadcasted_iota(jnp.int32, (chunk, chunk), 0)
          ^ lax.broadcasted_iota(jnp.int32, (chunk, chunk), 1))
    ps, qds, upds, decs = [], [], [], []
    for c in range(nch):
        sl = slice(c * chunk, (c + 1) * chunk)
        b2 = xw[:, c * dk:(c + 1) * dk]
        qc, kc = q[sl], kk[sl]
        qes, kes, halves = _hg_level_operands(qc, kc, fd[sl], b2, chunk)
        p = jnp.where(xr == 0, _dot_dims(qes[0].astype(BF16), kes[0].astype(BF16), NT_DIMS), 0.0)
        for lv in range(1, len(halves)):
            s = _dot_dims(qes[lv].astype(BF16), kes[lv].astype(BF16), NT_DIMS)
            p = jnp.where((xr >> int(math.log2(halves[lv]))) == 1, s, p)
        ps.append(p.astype(BF16))
        qds.append((qc * jnp.exp2(b2)).astype(BF16))
        b_last = b2[chunk - 1:chunk, :]
        k_dec = (kc * jnp.exp2(b_last - b2)).astype(BF16)
        upds.append(_dot_dims(v[sl], k_dec, TN_DIMS))
        decs.append(jnp.exp2(b_last))
    acc = acc + big(4) + big(5)

    st = jnp.where(first, 0.0, st_sc[head])
    outs = []
    for c in range(nch):
        sl = slice(c * chunk, (c + 1) * chunk)
        outs.append(_dot(ps[c], v[sl]) + _dot_dims(qds[c], st.astype(BF16), NT_DIMS))
        st = st * decs[c] + upds[c]
    st_sc[head] = st
    for k in range(6, MIX_K_PIECES):
        acc = acc + big(k)

    o = jnp.concatenate(outs, axis=0)
    o = o * lax.rsqrt(jnp.mean(o * o, axis=-1, keepdims=True) + NORM_EPS) * gain_ref[head]
    gp = zt_ref[:, 3 * dk:4 * dk].astype(F32)
    yb_ref[:, pl.ds(pl.multiple_of(head * dk, dk), dk)] = (o * (gp * jax.nn.sigmoid(gp))).astype(yb_ref.dtype)
    return acc


def _mix_kernel(*refs, n_cast, cast_perms, conv_width, n_rg, n_hg, chunk, blocks_per_seq):
    n_in = 13
    h_ref, g_ref, w_ref = refs[0:3]
    rg_prm = refs[3:10]
    hg_prm = refs[10:13]
    cast_src = refs[n_in:n_in + n_cast]
    zg_ref, ya_ref, yb_ref = refs[n_in + n_cast:n_in + n_cast + 3]
    cast_dst = refs[n_in + n_cast + 3:n_in + 2 * n_cast + 3]
    u_sc, zbuf, st_sc, tail_sc, carry_sc = refs[n_in + 2 * n_cast + 3:]

    m = pl.program_id(0)
    n = pl.program_id(1)
    first = (m % blocks_per_seq) == 0
    kp = h_ref.shape[1] // MIX_K_PIECES

    @pl.when(n == 0)
    def _():
        rows = h_ref.shape[0]
        step = min(NORM_ROWS, rows)

        def body(i, c):
            r = pl.ds(pl.multiple_of(i * step, step), step)
            x = h_ref[r, :]
            ms = jnp.mean(x * x, axis=-1, keepdims=True)
            u_sc[r, :] = (x * lax.rsqrt(ms + NORM_EPS) * g_ref[...]).astype(BF16)
            return c

        lax.fori_loop(0, rows // step, body, 0)

    def big(k):
        return _dot(u_sc[:, k * kp:(k + 1) * kp], w_ref[k * kp:(k + 1) * kp, :])

    def finish(acc):
        tile = acc.astype(BF16)
        zbuf[n % 2] = tile
        zg_ref[...] = tile

    prev = zbuf.at[(n + 1) % 2]
    rg_step = jnp.logical_and(n >= 1, n <= n_rg)
    hg_step = jnp.logical_and(n > n_rg, n <= n_rg + n_hg)

    @pl.when(rg_step)
    def _():
        finish(_rg_task(prev, n - 1, first, rg_prm, tail_sc, carry_sc, ya_ref, big, conv_width))

    @pl.when(hg_step)
    def _():
        finish(_hg_task(prev, n - 1 - n_rg, first, hg_prm, st_sc, yb_ref, big, chunk))

    @pl.when(jnp.logical_not(jnp.logical_or(rg_step, hg_step)))
    def _():
        acc = big(0)
        for k in range(1, MIX_K_PIECES):
            acc = acc + big(k)
        finish(acc)

    _do_casts(cast_src, cast_dst, cast_perms)


def _mix_inproj(h, gain, w, rg, hg, casts, cast_perms, *, batch, seq, d_rg, d_hg, d_model, tm, chunk):
    cw, cb, wa, ba, wx, bx, lam = rg
    lbc, hgain = hg
    T, D = h.shape
    N = w.shape[1]
    n_rg, blk = wa.shape[0], wa.shape[1]
    dk = HG_HEAD_DIM
    n_hg = d_hg // dk
    conv_width = cw.shape[0]
    tn = 2 * blk
    assert tn == 4 * dk and D % MIX_K_PIECES == 0 and N % tn == 0
    tm = _tile(seq, tm)
    chunk = min(chunk, tm)
    assert tm % chunk == 0 and chunk >= 2 * SUBLANES
    grid = (T // tm, N // tn)
    n_gate = grid[1] - n_rg - n_hg
    assert n_gate * tn == 2 * d_model and n_rg + n_hg + 1 <= grid[1]
    mc = jnp.asarray(_hg_constants(chunk), BF16)
    cwh = cw.reshape(conv_width, n_rg, blk).transpose(1, 0, 2)
    vec = lambda x: x.reshape(n_rg, 1, blk)
    params = [cwh, vec(cb), wa, vec(ba), wx, vec(bx), vec(lam), lbc, hgain.reshape(n_hg, 1, dk), mc]
    whole = lambda a: pl.BlockSpec(a.shape, lambda m, n, nd=a.ndim: (0,) * nd)
    c_in, c_out, c_shapes, c_ops = _cast_plan(casts, grid)
    est = (2 * tm * D * 4 + tm * D * 2 + 2 * D * tn * 2 + 4 * tm * tn * 2 + 2 * tm * tn * 4
           + 2 * tm * (d_rg + d_hg) * 2 + sum(2 * p.size * p.dtype.itemsize for p in params)
           + n_hg * dk * dk * 4 + 2 * n_rg * SUBLANES * blk * 4 + _cast_bytes(casts, grid))
    outs = pl.pallas_call(
        functools.partial(_mix_kernel, n_cast=len(casts), cast_perms=cast_perms, conv_width=conv_width,
                          n_rg=n_rg, n_hg=n_hg, chunk=chunk, blocks_per_seq=seq // tm),
        grid=grid,
        in_specs=[pl.BlockSpec((tm, D), lambda m, n: (m, 0)),
                  pl.BlockSpec((1, D), lambda m, n: (0, 0)),
                  pl.BlockSpec((D, tn), lambda m, n: (0, n))] + [whole(p) for p in params] + c_in,
        out_specs=[pl.BlockSpec((tm, tn), lambda m, n: (m, jnp.maximum(n - (n_rg + n_hg) + 1, 0))),
                   pl.BlockSpec((tm, d_rg), lambda m, n: (m, 0)),
                   pl.BlockSpec((tm, d_hg), lambda m, n: (m, 0))] + c_out,
        out_shape=[jax.ShapeDtypeStruct((T, (n_gate + 1) * tn), BF16),
                   jax.ShapeDtypeStruct((T, d_rg), BF16),
                   jax.ShapeDtypeStruct((T, d_hg), BF16)] + c_shapes,
        scratch_shapes=[pltpu.VMEM((tm, D), BF16),
                        pltpu.VMEM((2, tm, tn), BF16),
                        pltpu.VMEM((n_hg, dk, dk), F32),
                        pltpu.VMEM((n_rg, SUBLANES, blk), F32),
                        pltpu.VMEM((n_rg, SUBLANES, blk), F32)],
        compiler_params=pltpu.CompilerParams(
            dimension_semantics=("arbitrary", "arbitrary"),
            vmem_limit_bytes=_vmem_limit(est)),
        name="mix_inproj",
    )(h, gain.reshape(1, D), w, *params, *c_ops)
    return outs[0], outs[1], outs[2], outs[3:]


def kernel(x, ffn1_norm, ffn1_w_gate, ffn1_w_up, ffn1_w_down, mix_norm, w_in, conv_w, conv_b, rg_wa, rg_ba, rg_wx, rg_bx, rg_lambda, hg_lower_bounds, hg_out_norm, w_out_a, w_out_b, w_out, ffn2_norm, ffn2_w_gate, ffn2_w_up, ffn2_w_down, final_norm):
    batch, seq, d_model = x.shape
    depth = w_in.shape[0]
    d_rg = rg_lambda.shape[1]
    d_hg = hg_lower_bounds.shape[1]
    heads_hg = d_hg // HG_HEAD_DIM
    T = batch * seq
    tm, tn = ROW_TILE, COL_TILE

    lbc = _lb_constants(hg_lower_bounds)
    lbc = lbc.reshape(depth, 4, heads_hg, HG_HEAD_DIM).transpose(0, 2, 1, 3)

    bf = lambda w: w.astype(BF16)
    names = ("wg1", "wu1", "wd1", "w_in", "wa", "wb", "w_out", "wg2", "wu2", "wd2")
    stacks = dict(zip(names, (ffn1_w_gate, ffn1_w_up, ffn1_w_down, w_in, w_out_a, w_out_b, w_out,
                              ffn2_w_gate, ffn2_w_up, ffn2_w_down)))
    wt = {k: bf(v[0]) for k, v in stacks.items()}
    col_blocks = _mix_column_blocks(d_rg, d_hg, d_model, rg_wa.shape[2], HG_HEAD_DIM)
    wt["w_in"] = wt["w_in"].reshape(d_model, -1, LANES)[:, np.asarray(col_blocks), :].reshape(d_model, -1)
    rg_wa16, rg_wx16 = bf(rg_wa), bf(rg_wx)
    gate_tile = 2 * rg_wa.shape[2]

    h = x.reshape(T, d_model)
    for l in range(depth):
        nxt = {}

        def casts(*keys):
            return [(stacks[k], l + 1) for k in keys] if l + 1 < depth else []

        def keep(keys, outs):
            nxt.update(zip(keys, outs) if l + 1 < depth else ())

        act, c = _norm_mm(h, ffn1_norm[l], [wt["wg1"], wt["wu1"]], casts("wg1", "wu1"), tm=tm // 2, tn=tn)
        keep(("wg1", "wu1"), c)
        h, c = _mm_res(act, wt["wd1"], h, FFN_RES_WEIGHT, casts("wd1"), tm=tm, tn=tn, in_place=l > 0)
        keep(("wd1",), c)

        zg, y_a, y_b, c = _mix_inproj(
            h, mix_norm[l], wt["w_in"],
            (conv_w[l], conv_b[l], rg_wa16[l], rg_ba[l], rg_wx16[l], rg_bx[l], rg_lambda[l]),
            (lbc[l], hg_out_norm[l]), casts("w_in"), (col_blocks,),
            batch=batch, seq=seq, d_rg=d_rg, d_hg=d_hg, d_model=d_model, tm=tm // 2, chunk=HG_CHUNK)
        keep(("w_in",), c)
        merged, c = _merge(y_a, y_b, wt["wa"], wt["wb"], zg, gate_tile, gate_tile + d_model,
                           casts("wa", "wb"), tm=tm, tn=tn)
        keep(("wa", "wb"), c)
        h, c = _mm_res(merged, wt["w_out"], h, 1.0, casts("w_out"), tm=tm, tn=tn, in_place=True)
        keep(("w_out",), c)

        act, c = _norm_mm(h, ffn2_norm[l], [wt["wg2"], wt["wu2"]], casts("wg2", "wu2"), tm=tm // 2, tn=tn)
        keep(("wg2", "wu2"), c)
        h, c = _mm_res(act, wt["wd2"], h, FFN_RES_WEIGHT, casts("wd2"), tm=tm, tn=tn, in_place=True)
        keep(("wd2",), c)
        wt = nxt
    out = _rmsnorm(h, final_norm, tr=256)
    return out.reshape(batch, seq, d_model)
```

```python
import functools
import math

import numpy as np
import jax
import jax.numpy as jnp
from jax import lax
from jax.experimental import pallas as pl
from jax.experimental.pallas import tpu as pltpu

F32 = jnp.float32
BF16 = jnp.bfloat16

NORM_EPS = 1e-6
FFN_RES_WEIGHT = 0.5
RG_C = 8.0
HG_HEAD_DIM = 128

V7X_VMEM_BYTES = 64 * 1024 * 1024
SUBLANES = 8
LANES = 128
BF16_ROWS = 16

ROW_TILE = 1024
COL_TILE = 512
NORM_ROWS = 128
SEQ_TILE = 512
HG_SEQ_TILE = 1024
HG_SUB = 512
HG_CHUNK = 64

LOG2E = 1.4426950408889634
NT_DIMS = (((1,), (1,)), ((), ()))
TN_DIMS = (((0,), (0,)), ((), ()))


def _vmem_limit(block_bytes):
    return int(min(V7X_VMEM_BYTES - 4 * 1024 * 1024, block_bytes * 1.3 + 8 * 1024 * 1024))


def _tile(n, pref):
    t = min(pref, n)
    while n % t:
        t //= 2
    return t


def _dot(a, b):
    return jnp.dot(a, b, preferred_element_type=F32)


def _dot_dims(a, b, dims):
    return lax.dot_general(a, b, dims, preferred_element_type=F32)


def _cast_rows(rows, steps):
    pr = BF16_ROWS
    while rows % pr or rows // pr > steps:
        pr += BF16_ROWS
    return pr


def _cast_plan(casts, grid):
    steps = grid[0] * grid[1]
    in_specs, out_specs, out_shapes, operands = [], [], [], []
    for stack, layer in casts:
        _, R, C = stack.shape
        pr = _cast_rows(R, steps)
        pieces = R // pr

        def piece(m, n, pieces=pieces):
            return jnp.minimum(m * grid[1] + n, pieces - 1)

        in_specs.append(pl.BlockSpec((None, pr, C), lambda m, n, layer=layer, piece=piece: (layer, piece(m, n), 0)))
        out_specs.append(pl.BlockSpec((pr, C), lambda m, n, piece=piece: (piece(m, n), 0)))
        out_shapes.append(jax.ShapeDtypeStruct((R, C), BF16))
        operands.append(stack)
    return in_specs, out_specs, out_shapes, operands


def _cast_bytes(casts, grid):
    steps = grid[0] * grid[1]
    total = 0
    for stack, _ in casts:
        _, R, C = stack.shape
        total += 2 * _cast_rows(R, steps) * C * (4 + 2)
    return total


def _do_casts(src_refs, dst_refs):
    for s, d in zip(src_refs, dst_refs):
        d[...] = s[...].astype(BF16)


def _split_refs(refs, n_in, n_cast):
    ins = refs[:n_in]
    cast_src = refs[n_in:n_in + n_cast]
    out = refs[n_in + n_cast]
    cast_dst = refs[n_in + n_cast + 1:n_in + 2 * n_cast + 1]
    scratch = refs[n_in + 2 * n_cast + 1:]
    return ins, cast_src, out, cast_dst, scratch


def _norm_mm_kernel(*refs, n_w, n_cast):
    (h_ref, g_ref, *w_refs), cast_src, o_ref, cast_dst, (u_sc,) = _split_refs(refs, 2 + n_w, n_cast)

    @pl.when(pl.program_id(1) == 0)
    def _():
        rows = h_ref.shape[0]
        step = min(NORM_ROWS, rows)

        def body(i, c):
            r = pl.ds(pl.multiple_of(i * step, step), step)
            x = h_ref[r, :]
            ms = jnp.mean(x * x, axis=-1, keepdims=True)
            u_sc[r, :] = (x * lax.rsqrt(ms + NORM_EPS) * g_ref[...]).astype(BF16)
            return c

        lax.fori_loop(0, rows // step, body, 0)

    u = u_sc[...]
    if n_w == 2:
        g = _dot(u, w_refs[0][...])
        up = _dot(u, w_refs[1][...])
        o_ref[...] = (g * jax.nn.sigmoid(g) * up).astype(o_ref.dtype)
    else:
        o_ref[...] = _dot(u, w_refs[0][...]).astype(o_ref.dtype)
    _do_casts(cast_src, cast_dst)


def _norm_mm(h, gain, ws, casts, *, tm, tn):
    T, D = h.shape
    N = ws[0].shape[1]
    tm = _tile(T, tm)
    tn = _tile(N, tn)
    grid = (T // tm, N // tn)
    w_spec = pl.BlockSpec((D, tn), lambda m, n: (0, n))
    c_in, c_out, c_shapes, c_ops = _cast_plan(casts, grid)
    est = (2 * tm * D * 4 + tm * D * 2 + len(ws) * 2 * D * tn * 2 + 2 * tm * tn * 2
           + (1 + len(ws)) * tm * tn * 4 + _cast_bytes(casts, grid))
    outs = pl.pallas_call(
        functools.partial(_norm_mm_kernel, n_w=len(ws), n_cast=len(casts)),
        grid=grid,
        in_specs=[pl.BlockSpec((tm, D), lambda m, n: (m, 0)),
                  pl.BlockSpec((1, D), lambda m, n: (0, 0))] + [w_spec] * len(ws) + c_in,
        out_specs=[pl.BlockSpec((tm, tn), lambda m, n: (m, n))] + c_out,
        out_shape=[jax.ShapeDtypeStruct((T, N), BF16)] + c_shapes,
        scratch_shapes=[pltpu.VMEM((tm, D), BF16)],
        compiler_params=pltpu.CompilerParams(
            dimension_semantics=("arbitrary", "arbitrary"),
            vmem_limit_bytes=_vmem_limit(est)),
        name="norm_glu" if len(ws) == 2 else "norm_inproj",
    )(h, gain.reshape(1, D), *ws, *c_ops)
    return outs[0], outs[1:]


def _mm_res_kernel(*refs, scale, n_cast):
    (a_ref, w_ref, h_ref), cast_src, o_ref, cast_dst, _ = _split_refs(refs, 3, n_cast)
    acc = _dot(a_ref[...], w_ref[...])
    o_ref[...] = h_ref[...] + scale * acc
    _do_casts(cast_src, cast_dst)


def _mm_res(a, w, h, scale, casts, *, tm, tn, in_place):
    T, K = a.shape
    N = w.shape[1]
    tm = _tile(T, tm)
    tn = _tile(N, tn)
    grid = (T // tm, N // tn)
    c_in, c_out, c_shapes, c_ops = _cast_plan(casts, grid)
    est = 2 * tm * K * 2 + 2 * K * tn * 2 + 4 * tm * tn * 4 + tm * tn * 4 + _cast_bytes(casts, grid)
    outs = pl.pallas_call(
        functools.partial(_mm_res_kernel, scale=scale, n_cast=len(casts)),
        grid=grid,
        in_specs=[pl.BlockSpec((tm, K), lambda m, n: (m, 0)),
                  pl.BlockSpec((K, tn), lambda m, n: (0, n)),
                  pl.BlockSpec((tm, tn), lambda m, n: (m, n))] + c_in,
        out_specs=[pl.BlockSpec((tm, tn), lambda m, n: (m, n))] + c_out,
        out_shape=[jax.ShapeDtypeStruct((T, N), F32)] + c_shapes,
        input_output_aliases={2: 0} if in_place else {},
        compiler_params=pltpu.CompilerParams(
            dimension_semantics=("arbitrary", "arbitrary"),
            vmem_limit_bytes=_vmem_limit(est)),
        name="mm_residual",
    )(a, w, h, *c_ops)
    return outs[0], outs[1:]


def _merge_kernel(*refs, n_cast):
    (ya_ref, yb_ref, wa_ref, wb_ref, ga_ref, gb_ref), cast_src, o_ref, cast_dst, _ = _split_refs(refs, 6, n_cast)
    a = _dot(ya_ref[...], wa_ref[...])
    b = _dot(yb_ref[...], wb_ref[...])
    ga = jax.nn.sigmoid(ga_ref[...].astype(F32))
    gb = jax.nn.sigmoid(gb_ref[...].astype(F32))
    o_ref[...] = (ga * a + gb * b).astype(o_ref.dtype)
    _do_casts(cast_src, cast_dst)


def _merge(ya, yb, wa, wb, z, ga_col, gb_col, casts, *, tm, tn):
    T, KA = ya.shape
    KB = yb.shape[1]
    N = wa.shape[1]
    tm = _tile(T, tm)
    tn = _tile(N, tn)
    grid = (T // tm, N // tn)
    ga_blk, gb_blk = ga_col // tn, gb_col // tn
    assert ga_blk * tn == ga_col and gb_blk * tn == gb_col
    c_in, c_out, c_shapes, c_ops = _cast_plan(casts, grid)
    est = (2 * tm * (KA + KB) * 2 + 2 * (KA + KB) * tn * 2 + 6 * tm * tn * 2 + 4 * tm * tn * 4
           + _cast_bytes(casts, grid))
    outs = pl.pallas_call(
        functools.partial(_merge_kernel, n_cast=len(casts)),
        grid=grid,
        in_specs=[pl.BlockSpec((tm, KA), lambda m, n: (m, 0)),
                  pl.BlockSpec((tm, KB), lambda m, n: (m, 0)),
                  pl.BlockSpec((KA, tn), lambda m, n: (0, n)),
                  pl.BlockSpec((KB, tn), lambda m, n: (0, n)),
                  pl.BlockSpec((tm, tn), lambda m, n: (m, ga_blk + n)),
                  pl.BlockSpec((tm, tn), lambda m, n: (m, gb_blk + n))] + c_in,
        out_specs=[pl.BlockSpec((tm, tn), lambda m, n: (m, n))] + c_out,
        out_shape=[jax.ShapeDtypeStruct((T, N), BF16)] + c_shapes,
        compiler_params=pltpu.CompilerParams(
            dimension_semantics=("arbitrary", "arbitrary"),
            vmem_limit_bytes=_vmem_limit(est)),
        name="gated_merge",
    )(ya, yb, wa, wb, z, z, *c_ops)
    return outs[0], outs[1:]


def _rmsnorm_kernel(h_ref, g_ref, o_ref):
    x = h_ref[...]
    ms = jnp.mean(x * x, axis=-1, keepdims=True)
    o_ref[...] = x * lax.rsqrt(ms + NORM_EPS) * g_ref[...]


def _rmsnorm(h, gain, *, tr):
    T, D = h.shape
    tr = min(tr, T)
    return pl.pallas_call(
        _rmsnorm_kernel,
        grid=(T // tr,),
        in_specs=[pl.BlockSpec((tr, D), lambda i: (i, 0)),
                  pl.BlockSpec((1, D), lambda i: (0, 0))],
        out_specs=pl.BlockSpec((tr, D), lambda i: (i, 0)),
        out_shape=jax.ShapeDtypeStruct((T, D), F32),
        compiler_params=pltpu.CompilerParams(dimension_semantics=("parallel",)),
        name="final_norm",
    )(h, gain.reshape(1, D))


def _rglru_kernel(x_ref, y_ref, cw_ref, cb_ref, wa_ref, ba_ref, wx_ref, bx_ref, lam_ref,
                  o_ref, xp_sc, carry_sc, *, conv_width):
    tt, cb = x_ref.shape
    pad = SUBLANES

    @pl.when(pl.program_id(2) == 0)
    def _():
        xp_sc[...] = jnp.zeros((pad, cb), F32)
        carry_sc[...] = jnp.zeros((SUBLANES, cb), F32)

    x = x_ref[...].astype(F32)
    xpad = jnp.concatenate([xp_sc[...], x], axis=0)
    cw = cw_ref[0]
    xc = cb_ref[0] + x * cw[conv_width - 1:conv_width, :]
    for s in range(1, conv_width):
        xc = xc + pltpu.roll(xpad, s, 0)[pad:] * cw[conv_width - 1 - s:conv_width - s, :]
    xp_sc[...] = x[tt - pad:tt]

    xcb = xc.astype(BF16)
    r = jax.nn.sigmoid(_dot(xcb, wa_ref[0]) + ba_ref[0])
    i = jax.nn.sigmoid(_dot(xcb, wx_ref[0]) + bx_ref[0])
    log_a = (RG_C * jax.nn.log_sigmoid(lam_ref[0])) * r
    a = jnp.exp(log_a)
    b = jnp.sqrt(-jnp.tanh(log_a) * (a * a + 1.0)) * (i * xc)

    a = a.reshape(tt // SUBLANES, SUBLANES, cb)
    b = b.reshape(tt // SUBLANES, SUBLANES, cb)
    row8 = lax.broadcasted_iota(jnp.int32, a.shape, 1)
    for d in (1, 2, 4):
        keep = row8 >= d
        a_prev = jnp.where(keep, pltpu.roll(a, d, 1), 1.0)
        b_prev = jnp.where(keep, pltpu.roll(b, d, 1), 0.0)
        b = a * b_prev + b
        a = a * a_prev
    a = a.reshape(tt, cb)
    b = b.reshape(tt, cb)
    carry = carry_sc[...]
    hs = []
    for gi in range(tt // SUBLANES):
        g8 = slice(gi * SUBLANES, (gi + 1) * SUBLANES)
        hg = a[g8] * carry + b[g8]
        hs.append(hg)
        carry = jnp.broadcast_to(hg[SUBLANES - 1:SUBLANES, :], (SUBLANES, cb))
    carry_sc[...] = carry
    hseq = jnp.concatenate(hs, axis=0)
    o_ref[...] = (hseq * jax.nn.gelu(y_ref[...].astype(F32))).astype(o_ref.dtype)


def _rglru(z, cw, cb, wa, ba, wx, bx, lam, *, batch, seq, d_rg, tt):
    heads, blk = wa.shape[0], wa.shape[1]
    conv_width = cw.shape[0]
    tt = min(tt, seq)
    nt = seq // tt
    T = batch * seq
    vec = lambda v: v.reshape(heads, 1, blk)
    vspec = pl.BlockSpec((1, 1, blk), lambda b, h, t: (h, 0, 0))
    wspec = pl.BlockSpec((1, blk, blk), lambda b, h, t: (h, 0, 0))
    cwh = cw.reshape(conv_width, heads, blk).transpose(1, 0, 2)
    return pl.pallas_call(
        functools.partial(_rglru_kernel, conv_width=conv_width),
        grid=(batch, heads, nt),
        in_specs=[pl.BlockSpec((tt, blk), lambda b, h, t: (b * nt + t, h)),
                  pl.BlockSpec((tt, blk), lambda b, h, t: (b * nt + t, heads + h)),
                  pl.BlockSpec((1, conv_width, blk), lambda b, h, t: (h, 0, 0)),
                  vspec, wspec, vspec, wspec, vspec, vspec],
        out_specs=pl.BlockSpec((tt, blk), lambda b, h, t: (b * nt + t, h)),
        out_shape=jax.ShapeDtypeStruct((T, d_rg), BF16),
        scratch_shapes=[pltpu.VMEM((SUBLANES, blk), F32),
                        pltpu.VMEM((SUBLANES, blk), F32)],
        compiler_params=pltpu.CompilerParams(
            dimension_semantics=("parallel", "parallel", "arbitrary")),
        name="rglru",
    )(z, z, cwh, vec(cb), wa, vec(ba), wx, vec(bx), vec(lam))


def _hg_constants(chunk):
    t = np.arange(chunk)[:, None]
    r = np.arange(chunk)[None, :]
    tril = (r <= t).astype(np.float32)
    return np.concatenate([tril, tril, tril], axis=1)


def _hg_masks(chunk):
    t = np.arange(chunk)[:, None]
    s = np.arange(chunk)[None, :]
    masks = [(t == s)]
    m = chunk // 4
    while m >= 1:
        masks.append(((t ^ s) >> int(math.log2(m))) == 1)
        m //= 2
    return np.stack(masks).astype(np.float32)


def _split3(x):
    hi = x.astype(BF16)
    r1 = x - hi.astype(F32)
    mid = r1.astype(BF16)
    lo = (r1 - mid.astype(F32)).astype(BF16)
    return hi, mid, lo


def _silu(x):
    return x * (1.0 / (1.0 + jnp.exp2(x * (-LOG2E))))


def _hg_level_operands(q, kk, fd, b2, chunk):
    dk = q.shape[1]
    zeros = lambda n: jnp.zeros((n, dk), F32)
    qes, kes, halves = [q], [kk], [0]
    m = chunk // 2
    while m >= SUBLANES:
        qparts, kparts = [], []
        for s in range(0, chunk, 2 * m):
            ref = b2[s + m - 1:s + m, :]
            qparts += [zeros(m), q[s + m:s + 2 * m] * jnp.exp2(b2[s + m:s + 2 * m] - ref)]
            kparts += [kk[s:s + m] * jnp.exp2(ref - b2[s:s + m]), zeros(m)]
        qes.append(jnp.concatenate(qparts, axis=0))
        kes.append(jnp.concatenate(kparts, axis=0))
        halves.append(m)
        m //= 2
    row = lax.broadcasted_iota(jnp.int32, (chunk, dk), 0)
    b3 = b2.reshape(chunk // SUBLANES, SUBLANES, dk)
    ref = jnp.broadcast_to(b3[:, 3:4, :], b3.shape).reshape(chunk, dk)
    w = jnp.exp2(-jnp.abs(b2 - ref))
    upper = (row & 4) != 0
    qes.append(jnp.where(upper, q * w, 0.0))
    kes.append(jnp.where(upper, 0.0, kk * w))
    halves.append(4)
    f3 = fd.reshape(chunk // SUBLANES, SUBLANES, dk)
    f_prev = pltpu.roll(f3, 1, 1).reshape(chunk, dk)
    f_next = pltpu.roll(f3, SUBLANES - 1, 1).reshape(chunk, dk)
    r4 = row & 3
    qes.append(jnp.where(r4 >= 2, q * jnp.where(r4 == 3, fd * f_prev, fd), 0.0))
    kes.append(jnp.where(r4 >= 2, 0.0, kk * jnp.where(r4 == 0, f_next, 1.0)))
    halves.append(2)
    odd = (row & 1) != 0
    qes.append(jnp.where(odd, q * fd, 0.0))
    kes.append(jnp.where(odd, 0.0, kk))
    halves.append(1)
    return qes, kes, halves


def _hg_core(zq, zf, v, zg, lbc, gain, mc, pm_ref, st, chunk):
    rows, dk = zq.shape
    nch = rows // chunk
    log_lb, log_1m_lb, one_m_lb, lb = lbc[0:1], lbc[1:2], lbc[2:3], lbc[3:4]
    zf = zf.astype(F32)
    e = jnp.exp2(jnp.abs(zf) * (-LOG2E))
    den = 1.0 + e
    inv = 1.0 / den
    t = one_m_lb * jnp.where(zf >= 0.0, inv, e * inv)
    fd = lb + t
    kk = one_m_lb - t
    x2 = log_1m_lb + (jnp.minimum(zf, 0.0) - jnp.log(den))
    lf2 = (jnp.maximum(log_lb, x2)
           + jnp.log(1.0 + jnp.exp2(jnp.abs(log_lb - x2) * (-LOG2E)))) * LOG2E
    q = _silu(zq.astype(F32))

    w3 = jnp.concatenate(
        [jnp.concatenate([p[c * chunk:(c + 1) * chunk] for c in range(nch)], axis=1)
         for p in _split3(lf2)], axis=0)
    xw = _dot(mc, w3)

    ps, qds, upds, decs = [], [], [], []
    for c in range(nch):
        sl = slice(c * chunk, (c + 1) * chunk)
        b2 = xw[:, c * dk:(c + 1) * dk]
        qc, kc = q[sl], kk[sl]
        qes, kes, halves = _hg_level_operands(qc, kc, fd[sl], b2, chunk)
        p = _dot_dims(qes[1].astype(BF16), kes[1].astype(BF16), NT_DIMS)
        for lv in [0] + list(range(2, len(halves))):
            s = _dot_dims(qes[lv].astype(BF16), kes[lv].astype(BF16), NT_DIMS)
            p = p + s * pm_ref[0 if lv == 0 else lv - 1]
        ps.append(p.astype(BF16))
        qds.append((qc * jnp.exp2(b2)).astype(BF16))
        b_last = b2[chunk - 1:chunk, :]
        k_dec = (kc * jnp.exp2(b_last - b2)).astype(BF16)
        upds.append(_dot_dims(v[sl], k_dec, TN_DIMS))
        decs.append(jnp.exp2(b_last))

    outs = []
    for c in range(nch):
        sl = slice(c * chunk, (c + 1) * chunk)
        outs.append(_dot(ps[c], v[sl]) + _dot_dims(qds[c], st.astype(BF16), NT_DIMS))
        st = st * decs[c] + upds[c]

    o = jnp.concatenate(outs, axis=0)
    o = o * lax.rsqrt(jnp.mean(o * o, axis=-1, keepdims=True) + NORM_EPS) * gain
    return o * _silu(zg.astype(F32)), st


def _hgrn2_kernel(q_ref, f_ref, i_ref, g_ref, lbc_ref, gain_ref, mc_ref, pm_ref, o_ref, st_sc, *, chunk, sub):
    tt, dk = q_ref.shape

    @pl.when(pl.program_id(2) == 0)
    def _():
        st_sc[...] = jnp.zeros(st_sc.shape, F32)

    def body(j, carry):
        rows = pl.ds(pl.multiple_of(j * sub, sub), sub)
        y, st = _hg_core(q_ref[rows, :], f_ref[rows, :], i_ref[rows, :], g_ref[rows, :], lbc_ref[0],
                         gain_ref[0], mc_ref[...], pm_ref, st_sc[...], chunk)
        st_sc[...] = st
        o_ref[rows, :] = y.astype(o_ref.dtype)
        return carry

    lax.fori_loop(0, tt // sub, body, 0)


def _hgrn2(z, lbc, gain, *, batch, seq, d_hg, col0, tt, sub, chunk):
    dk = HG_HEAD_DIM
    heads = d_hg // dk
    tt = min(tt, seq)
    sub = min(sub, tt)
    chunk = min(chunk, sub)
    assert chunk >= 2 * SUBLANES and tt % sub == 0 and sub % chunk == 0
    nt = seq // tt
    T = batch * seq
    mc = jnp.asarray(_hg_constants(chunk), BF16)
    pm = jnp.asarray(_hg_masks(chunk), F32)
    base = col0 // dk

    def zspec(j):
        return pl.BlockSpec((tt, dk), lambda b, h, t: (b * nt + t, base + j * heads + h))

    return pl.pallas_call(
        functools.partial(_hgrn2_kernel, chunk=chunk, sub=sub),
        grid=(batch, heads, nt),
        in_specs=[zspec(0), zspec(1), zspec(2), zspec(3),
                  pl.BlockSpec((1, 4, dk), lambda b, h, t: (h, 0, 0)),
                  pl.BlockSpec((1, 1, dk), lambda b, h, t: (h, 0, 0)),
                  pl.BlockSpec(mc.shape, lambda b, h, t: (0, 0)),
                  pl.BlockSpec(pm.shape, lambda b, h, t: (0, 0, 0))],
        out_specs=pl.BlockSpec((tt, dk), lambda b, h, t: (b * nt + t, h)),
        out_shape=jax.ShapeDtypeStruct((T, d_hg), BF16),
        scratch_shapes=[pltpu.VMEM((dk, dk), F32)],
        compiler_params=pltpu.CompilerParams(
            dimension_semantics=("parallel", "parallel", "arbitrary")),
        name="hgrn2",
    )(z, z, z, z, lbc, gain.reshape(heads, 1, dk), mc, pm)


def _lb_kernel(p_ref, o_ref):
    depth = p_ref.shape[0]
    p = p_ref[...]
    mx = jnp.max(p, axis=0, keepdims=True)
    ex = jnp.exp(p - mx)
    sm = ex / jnp.sum(ex, axis=0, keepdims=True)
    c0 = sm[0:1, :]
    c = c0
    for l in range(depth):
        if l > 0:
            c = c + sm[l:l + 1, :]
        lb = c - c0
        o_ref[l, 0:1, :] = jnp.log(lb)
        o_ref[l, 1:2, :] = jnp.log1p(-lb)
        o_ref[l, 2:3, :] = 1.0 - lb
        o_ref[l, 3:4, :] = lb


def _lb_constants(lb_param):
    depth, d_hg = lb_param.shape
    return pl.pallas_call(
        _lb_kernel,
        out_shape=jax.ShapeDtypeStruct((depth, 4, d_hg), F32),
        name="hgrn2_lower_bounds",
    )(lb_param)


def kernel(x, ffn1_norm, ffn1_w_gate, ffn1_w_up, ffn1_w_down, mix_norm, w_in, conv_w, conv_b, rg_wa, rg_ba, rg_wx, rg_bx, rg_lambda, hg_lower_bounds, hg_out_norm, w_out_a, w_out_b, w_out, ffn2_norm, ffn2_w_gate, ffn2_w_up, ffn2_w_down, final_norm):
    batch, seq, d_model = x.shape
    depth = w_in.shape[0]
    d_rg = rg_lambda.shape[1]
    d_hg = hg_lower_bounds.shape[1]
    heads_hg = d_hg // HG_HEAD_DIM
    T = batch * seq
    tm, tn = ROW_TILE, COL_TILE

    lbc = _lb_constants(hg_lower_bounds)
    lbc = lbc.reshape(depth, 4, heads_hg, HG_HEAD_DIM).transpose(0, 2, 1, 3)

    bf = lambda w: w.astype(BF16)
    names = ("wg1", "wu1", "wd1", "w_in", "wa", "wb", "w_out", "wg2", "wu2", "wd2")
    stacks = dict(zip(names, (ffn1_w_gate, ffn1_w_up, ffn1_w_down, w_in, w_out_a, w_out_b, w_out,
                              ffn2_w_gate, ffn2_w_up, ffn2_w_down)))
    wt = {(k, 0): bf(stacks[k][0]) for k in ("wg1", "wu1", "wd1")}
    rg_wa16, rg_wx16 = bf(rg_wa), bf(rg_wx)

    def run(fn, to_cast, *args, **kw):
        todo = [(k, ll) for k, ll in to_cast if ll < depth and (k, ll) not in wt]
        out, cast = fn(*args, [(stacks[k], ll) for k, ll in todo], **kw)
        wt.update(zip(todo, cast))
        return out

    h = x.reshape(T, d_model)
    for l in range(depth):
        early = [("w_in", l), ("wa", l), ("wb", l)]
        act = run(_norm_mm, [("wg1", l + 1), ("wu1", l + 1)] + early,
                  h, ffn1_norm[l], [wt["wg1", l], wt["wu1", l]], tm=tm // 2, tn=tn)
        early = [("w_out", l), ("wg2", l), ("wu2", l), ("wd2", l)]
        h = run(_mm_res, [("wd1", l + 1)] + early,
                act, wt["wd1", l], h, FFN_RES_WEIGHT, tm=tm, tn=tn, in_place=l > 0)

        z = run(_norm_mm, [("w_in", l + 1)], h, mix_norm[l], [wt["w_in", l]], tm=tm // 2, tn=2 * tn)
        y_a = _rglru(z, conv_w[l], conv_b[l], rg_wa16[l], rg_ba[l], rg_wx16[l], rg_bx[l], rg_lambda[l],
                     batch=batch, seq=seq, d_rg=d_rg, tt=SEQ_TILE)
        y_b = _hgrn2(z, lbc[l], hg_out_norm[l], batch=batch, seq=seq, d_hg=d_hg,
                     col0=2 * d_rg, tt=HG_SEQ_TILE, sub=HG_SUB, chunk=HG_CHUNK)
        ga_col = 2 * d_rg + 4 * d_hg
        merged = run(_merge, [("wa", l + 1), ("wb", l + 1)], y_a, y_b, wt["wa", l], wt["wb", l], z,
                     ga_col, ga_col + d_model, tm=tm, tn=tn)
        h = run(_mm_res, [("w_out", l + 1)], merged, wt["w_out", l], h, 1.0, tm=tm, tn=tn, in_place=True)

        act = run(_norm_mm, [("wg2", l + 1), ("wu2", l + 1)],
                  h, ffn2_norm[l], [wt["wg2", l], wt["wu2", l]], tm=tm // 2, tn=tn)
        h = run(_mm_res, [("wd2", l + 1)], act, wt["wd2", l], h, FFN_RES_WEIGHT, tm=tm, tn=tn, in_place=True)
    out = _rmsnorm(h, final_norm, tr=256)
    return out.reshape(batch, seq, d_model)
```

```python
import functools
import math

import numpy as np
import jax
import jax.numpy as jnp
from jax import lax
from jax.experimental import pallas as pl
from jax.experimental.pallas import tpu as pltpu

F32 = jnp.float32
BF16 = jnp.bfloat16

NORM_EPS = 1e-6
FFN_RES_WEIGHT = 0.5
RG_C = 8.0
HG_HEAD_DIM = 128

V7X_VMEM_BYTES = 64 * 1024 * 1024
SUBLANES = 8
LANES = 128
BF16_ROWS = 16

ROW_TILE = 1024
COL_TILE = 512
NORM_ROWS = 128
SEQ_TILE = 512
HG_SEQ_TILE = 1024
HG_SUB = 1024
HG_CHUNK = 64

LOG2E = 1.4426950408889634
NT_DIMS = (((1,), (1,)), ((), ()))
TN_DIMS = (((0,), (0,)), ((), ()))


def _vmem_limit(block_bytes):
    return int(min(V7X_VMEM_BYTES - 4 * 1024 * 1024, block_bytes * 1.3 + 8 * 1024 * 1024))


def _tile(n, pref):
    t = min(pref, n)
    while n % t:
        t //= 2
    return t


def _dot(a, b):
    return jnp.dot(a, b, preferred_element_type=F32)


def _dot_dims(a, b, dims):
    return lax.dot_general(a, b, dims, preferred_element_type=F32)


def _sigmoid(x):
    return 1.0 / (1.0 + jnp.exp2(x * (-LOG2E)))


def _silu(x):
    return x * _sigmoid(x)


def _norm_rows(h_ref, g_ref, u_ref, rows):
    D = h_ref.shape[1]
    ssq = None
    for c in range(0, D, D // 2):
        x = h_ref[rows, c:c + D // 2]
        part = jnp.sum(x * x, axis=-1, keepdims=True)
        ssq = part if ssq is None else ssq + part
    scale = lax.rsqrt(ssq * (1.0 / D) + NORM_EPS)
    for c in range(0, D, D // 4):
        cols = slice(c, c + D // 4)
        u_ref[rows, cols] = (h_ref[rows, cols] * scale * g_ref[:, cols]).astype(BF16)


def _cast_rows(rows, steps):
    pr = BF16_ROWS
    while rows % pr or rows // pr > steps:
        pr += BF16_ROWS
    return pr


def _cast_plan(casts, grid):
    steps = grid[0] * grid[1]
    in_specs, out_specs, out_shapes, operands = [], [], [], []
    for stack, layer in casts:
        _, R, C = stack.shape
        pr = _cast_rows(R, steps)
        pieces = R // pr

        def piece(m, n, pieces=pieces):
            return jnp.minimum(m * grid[1] + n, pieces - 1)

        in_specs.append(pl.BlockSpec((None, pr, C), lambda m, n, layer=layer, piece=piece: (layer, piece(m, n), 0)))
        out_specs.append(pl.BlockSpec((pr, C), lambda m, n, piece=piece: (piece(m, n), 0)))
        out_shapes.append(jax.ShapeDtypeStruct((R, C), BF16))
        operands.append(stack)
    return in_specs, out_specs, out_shapes, operands


def _cast_bytes(casts, grid):
    steps = grid[0] * grid[1]
    total = 0
    for stack, _ in casts:
        _, R, C = stack.shape
        total += 2 * _cast_rows(R, steps) * C * (4 + 2)
    return total


def _do_casts(src_refs, dst_refs):
    for s, d in zip(src_refs, dst_refs):
        d[...] = s[...].astype(BF16)


def _split_refs(refs, n_in, n_cast):
    ins = refs[:n_in]
    cast_src = refs[n_in:n_in + n_cast]
    out = refs[n_in + n_cast]
    cast_dst = refs[n_in + n_cast + 1:n_in + 2 * n_cast + 1]
    scratch = refs[n_in + 2 * n_cast + 1:]
    return ins, cast_src, out, cast_dst, scratch


def _norm_mm_kernel(*refs, n_w, n_cast):
    (h_ref, g_ref, *w_refs), cast_src, o_ref, cast_dst, (u_sc,) = _split_refs(refs, 2 + n_w, n_cast)

    @pl.when(pl.program_id(1) == 0)
    def _():
        rows = h_ref.shape[0]
        step = min(NORM_ROWS, rows)

        def body(i, c):
            _norm_rows(h_ref, g_ref, u_sc, pl.ds(pl.multiple_of(i * step, step), step))
            return c

        lax.fori_loop(0, rows // step, body, 0)

    u = u_sc[...]
    if n_w == 2:
        g = _dot(u, w_refs[0][...])
        up = _dot(u, w_refs[1][...])
        o_ref[...] = (_silu(g) * up).astype(o_ref.dtype)
    else:
        o_ref[...] = _dot(u, w_refs[0][...]).astype(o_ref.dtype)
    _do_casts(cast_src, cast_dst)


def _norm_mm(h, gain, ws, casts, *, tm, tn):
    T, D = h.shape
    N = ws[0].shape[1]
    tm = _tile(T, tm)
    tn = _tile(N, tn)
    grid = (T // tm, N // tn)
    w_spec = pl.BlockSpec((D, tn), lambda m, n: (0, n))
    c_in, c_out, c_shapes, c_ops = _cast_plan(casts, grid)
    est = (2 * tm * D * 4 + tm * D * 2 + len(ws) * 2 * D * tn * 2 + 2 * tm * tn * 2
           + (1 + len(ws)) * tm * tn * 4 + _cast_bytes(casts, grid))
    outs = pl.pallas_call(
        functools.partial(_norm_mm_kernel, n_w=len(ws), n_cast=len(casts)),
        grid=grid,
        in_specs=[pl.BlockSpec((tm, D), lambda m, n: (m, 0)),
                  pl.BlockSpec((1, D), lambda m, n: (0, 0))] + [w_spec] * len(ws) + c_in,
        out_specs=[pl.BlockSpec((tm, tn), lambda m, n: (m, n))] + c_out,
        out_shape=[jax.ShapeDtypeStruct((T, N), BF16)] + c_shapes,
        scratch_shapes=[pltpu.VMEM((tm, D), BF16)],
        compiler_params=pltpu.CompilerParams(
            dimension_semantics=("arbitrary", "arbitrary"),
            vmem_limit_bytes=_vmem_limit(est)),
        name="norm_glu" if len(ws) == 2 else "norm_inproj",
    )(h, gain.reshape(1, D), *ws, *c_ops)
    return outs[0], outs[1:]


def _mm_res_kernel(*refs, scale, n_cast):
    (a_ref, w_ref, h_ref), cast_src, o_ref, cast_dst, _ = _split_refs(refs, 3, n_cast)
    acc = _dot(a_ref[...], w_ref[...])
    o_ref[...] = h_ref[...] + scale * acc
    _do_casts(cast_src, cast_dst)


def _mm_res(a, w, h, scale, casts, *, tm, tn, in_place):
    T, K = a.shape
    N = w.shape[1]
    tm = _tile(T, tm)
    tn = _tile(N, tn)
    grid = (T // tm, N // tn)
    c_in, c_out, c_shapes, c_ops = _cast_plan(casts, grid)
    est = 2 * tm * K * 2 + 2 * K * tn * 2 + 4 * tm * tn * 4 + tm * tn * 4 + _cast_bytes(casts, grid)
    outs = pl.pallas_call(
        functools.partial(_mm_res_kernel, scale=scale, n_cast=len(casts)),
        grid=grid,
        in_specs=[pl.BlockSpec((tm, K), lambda m, n: (m, 0)),
                  pl.BlockSpec((K, tn), lambda m, n: (0, n)),
                  pl.BlockSpec((tm, tn), lambda m, n: (m, n))] + c_in,
        out_specs=[pl.BlockSpec((tm, tn), lambda m, n: (m, n))] + c_out,
        out_shape=[jax.ShapeDtypeStruct((T, N), F32)] + c_shapes,
        input_output_aliases={2: 0} if in_place else {},
        compiler_params=pltpu.CompilerParams(
            dimension_semantics=("arbitrary", "arbitrary"),
            vmem_limit_bytes=_vmem_limit(est)),
        name="mm_residual",
    )(a, w, h, *c_ops)
    return outs[0], outs[1:]


def _merge_kernel(*refs, n_cast):
    (ya_ref, yb_ref, wa_ref, wb_ref, ga_ref, gb_ref), cast_src, o_ref, cast_dst, _ = _split_refs(refs, 6, n_cast)
    a = _dot(ya_ref[...], wa_ref[...])
    b = _dot(yb_ref[...], wb_ref[...])
    ga = _sigmoid(ga_ref[...].astype(F32))
    gb = _sigmoid(gb_ref[...].astype(F32))
    o_ref[...] = (ga * a + gb * b).astype(o_ref.dtype)
    _do_casts(cast_src, cast_dst)


def _merge(ya, yb, wa, wb, z, ga_col, gb_col, casts, *, tm, tn):
    T, KA = ya.shape
    KB = yb.shape[1]
    N = wa.shape[1]
    tm = _tile(T, tm)
    tn = _tile(N, tn)
    grid = (T // tm, N // tn)
    ga_blk, gb_blk = ga_col // tn, gb_col // tn
    assert ga_blk * tn == ga_col and gb_blk * tn == gb_col
    c_in, c_out, c_shapes, c_ops = _cast_plan(casts, grid)
    est = (2 * tm * (KA + KB) * 2 + 2 * (KA + KB) * tn * 2 + 6 * tm * tn * 2 + 4 * tm * tn * 4
           + _cast_bytes(casts, grid))
    outs = pl.pallas_call(
        functools.partial(_merge_kernel, n_cast=len(casts)),
        grid=grid,
        in_specs=[pl.BlockSpec((tm, KA), lambda m, n: (m, 0)),
                  pl.BlockSpec((tm, KB), lambda m, n: (m, 0)),
                  pl.BlockSpec((KA, tn), lambda m, n: (0, n)),
                  pl.BlockSpec((KB, tn), lambda m, n: (0, n)),
                  pl.BlockSpec((tm, tn), lambda m, n: (m, ga_blk + n)),
                  pl.BlockSpec((tm, tn), lambda m, n: (m, gb_blk + n))] + c_in,
        out_specs=[pl.BlockSpec((tm, tn), lambda m, n: (m, n))] + c_out,
        out_shape=[jax.ShapeDtypeStruct((T, N), BF16)] + c_shapes,
        compiler_params=pltpu.CompilerParams(
            dimension_semantics=("arbitrary", "arbitrary"),
            vmem_limit_bytes=_vmem_limit(est)),
        name="gated_merge",
    )(ya, yb, wa, wb, z, z, *c_ops)
    return outs[0], outs[1:]


def _rmsnorm_kernel(h_ref, g_ref, o_ref):
    x = h_ref[...]
    ms = jnp.mean(x * x, axis=-1, keepdims=True)
    o_ref[...] = x * lax.rsqrt(ms + NORM_EPS) * g_ref[...]


def _rmsnorm(h, gain, *, tr):
    T, D = h.shape
    tr = min(tr, T)
    return pl.pallas_call(
        _rmsnorm_kernel,
        grid=(T // tr,),
        in_specs=[pl.BlockSpec((tr, D), lambda i: (i, 0)),
                  pl.BlockSpec((1, D), lambda i: (0, 0))],
        out_specs=pl.BlockSpec((tr, D), lambda i: (i, 0)),
        out_shape=jax.ShapeDtypeStruct((T, D), F32),
        compiler_params=pltpu.CompilerParams(dimension_semantics=("parallel",)),
        name="final_norm",
    )(h, gain.reshape(1, D))


def _rglru_kernel(x_ref, y_ref, cw_ref, cb_ref, wa_ref, ba_ref, wx_ref, bx_ref, lam_ref,
                  o_ref, xp_sc, carry_sc, *, conv_width):
    tt, cb = x_ref.shape
    pad = SUBLANES

    @pl.when(pl.program_id(2) == 0)
    def _():
        xp_sc[...] = jnp.zeros((pad, cb), F32)
        carry_sc[...] = jnp.zeros((SUBLANES, cb), F32)

    x = x_ref[...].astype(F32)
    xpad = jnp.concatenate([xp_sc[...], x], axis=0)
    cw = cw_ref[0]
    xc = cb_ref[0] + x * cw[conv_width - 1:conv_width, :]
    for s in range(1, conv_width):
        xc = xc + pltpu.roll(xpad, s, 0)[pad:] * cw[conv_width - 1 - s:conv_width - s, :]
    xp_sc[...] = x[tt - pad:tt]

    xcb = xc.astype(BF16)
    r = _sigmoid(_dot(xcb, wa_ref[0]) + ba_ref[0])
    i = _sigmoid(_dot(xcb, wx_ref[0]) + bx_ref[0])
    log_a = (RG_C * jax.nn.log_sigmoid(lam_ref[0])) * r
    a = jnp.exp(log_a)
    b = jnp.sqrt(-jnp.tanh(log_a) * (a * a + 1.0)) * (i * xc)

    a = a.reshape(tt // SUBLANES, SUBLANES, cb)
    b = b.reshape(tt // SUBLANES, SUBLANES, cb)
    row8 = lax.broadcasted_iota(jnp.int32, a.shape, 1)
    for d in (1, 2, 4):
        keep = row8 >= d
        a_prev = jnp.where(keep, pltpu.roll(a, d, 1), 1.0)
        b_prev = jnp.where(keep, pltpu.roll(b, d, 1), 0.0)
        b = a * b_prev + b
        a = a * a_prev
    a = a.reshape(tt, cb)
    b = b.reshape(tt, cb)
    carry = carry_sc[...]
    hs = []
    for gi in range(tt // SUBLANES):
        g8 = slice(gi * SUBLANES, (gi + 1) * SUBLANES)
        hg = a[g8] * carry + b[g8]
        hs.append(hg)
        carry = jnp.broadcast_to(hg[SUBLANES - 1:SUBLANES, :], (SUBLANES, cb))
    carry_sc[...] = carry
    hseq = jnp.concatenate(hs, axis=0)
    o_ref[...] = (hseq * jax.nn.gelu(y_ref[...].astype(F32))).astype(o_ref.dtype)


def _rglru(z, cw, cb, wa, ba, wx, bx, lam, *, batch, seq, d_rg, tt):
    heads, blk = wa.shape[0], wa.shape[1]
    conv_width = cw.shape[0]
    tt = min(tt, seq)
    nt = seq // tt
    T = batch * seq
    vec = lambda v: v.reshape(heads, 1, blk)
    vspec = pl.BlockSpec((1, 1, blk), lambda b, h, t: (h, 0, 0))
    wspec = pl.BlockSpec((1, blk, blk), lambda b, h, t: (h, 0, 0))
    cwh = cw.reshape(conv_width, heads, blk).transpose(1, 0, 2)
    return pl.pallas_call(
        functools.partial(_rglru_kernel, conv_width=conv_width),
        grid=(batch, heads, nt),
        in_specs=[pl.BlockSpec((tt, blk), lambda b, h, t: (b * nt + t, h)),
                  pl.BlockSpec((tt, blk), lambda b, h, t: (b * nt + t, heads + h)),
                  pl.BlockSpec((1, conv_width, blk), lambda b, h, t: (h, 0, 0)),
                  vspec, wspec, vspec, wspec, vspec, vspec],
        out_specs=pl.BlockSpec((tt, blk), lambda b, h, t: (b * nt + t, h)),
        out_shape=jax.ShapeDtypeStruct((T, d_rg), BF16),
        scratch_shapes=[pltpu.VMEM((SUBLANES, blk), F32),
                        pltpu.VMEM((SUBLANES, blk), F32)],
        compiler_params=pltpu.CompilerParams(
            dimension_semantics=("parallel", "parallel", "arbitrary")),
        name="rglru",
    )(z, z, cwh, vec(cb), wa, vec(ba), wx, vec(bx), vec(lam))


def _hg_constants(chunk):
    t = np.arange(chunk)[:, None]
    r = np.arange(chunk)[None, :]
    tril = (r <= t).astype(np.float32)
    return np.concatenate([tril, tril, tril], axis=1)


def _hg_masks(chunk):
    t = np.arange(chunk)[:, None]
    s = np.arange(chunk)[None, :]
    masks = [(t == s)]
    m = chunk // 4
    while m >= 1:
        masks.append(((t ^ s) >> int(math.log2(m))) == 1)
        m //= 2
    return np.stack(masks).astype(np.float32)


def _split3(x):
    hi = x.astype(BF16)
    r1 = x - hi.astype(F32)
    mid = r1.astype(BF16)
    lo = (r1 - mid.astype(F32)).astype(BF16)
    return hi, mid, lo


def _hg_level_operands(q, kk, fd, b2, chunk):
    dk = q.shape[1]
    zeros = lambda n: jnp.zeros((n, dk), F32)
    qes, kes, halves = [q], [kk], [0]
    m = chunk // 2
    while m >= SUBLANES:
        qparts, kparts = [], []
        for s in range(0, chunk, 2 * m):
            ref = b2[s + m - 1:s + m, :]
            qparts += [zeros(m), q[s + m:s + 2 * m] * jnp.exp2(b2[s + m:s + 2 * m] - ref)]
            kparts += [kk[s:s + m] * jnp.exp2(ref - b2[s:s + m]), zeros(m)]
        qes.append(jnp.concatenate(qparts, axis=0))
        kes.append(jnp.concatenate(kparts, axis=0))
        halves.append(m)
        m //= 2
    row = lax.broadcasted_iota(jnp.int32, (chunk, dk), 0)
    b3 = b2.reshape(chunk // SUBLANES, SUBLANES, dk)
    ref = jnp.broadcast_to(b3[:, 3:4, :], b3.shape).reshape(chunk, dk)
    w = jnp.exp2(-jnp.abs(b2 - ref))
    upper = (row & 4) != 0
    qes.append(jnp.where(upper, q * w, 0.0))
    kes.append(jnp.where(upper, 0.0, kk * w))
    halves.append(4)
    f3 = fd.reshape(chunk // SUBLANES, SUBLANES, dk)
    f_prev = pltpu.roll(f3, 1, 1).reshape(chunk, dk)
    f_next = pltpu.roll(f3, SUBLANES - 1, 1).reshape(chunk, dk)
    r4 = row & 3
    qes.append(jnp.where(r4 >= 2, q * jnp.where(r4 == 3, fd * f_prev, fd), 0.0))
    kes.append(jnp.where(r4 >= 2, 0.0, kk * jnp.where(r4 == 0, f_next, 1.0)))
    halves.append(2)
    odd = (row & 1) != 0
    qes.append(jnp.where(odd, q * fd, 0.0))
    kes.append(jnp.where(odd, 0.0, kk))
    halves.append(1)
    return qes, kes, halves


def _hg_core(zq, zf, v, zg, lbc, gain, mc, pm_ref, st, chunk):
    rows, dk = zq.shape
    nch = rows // chunk
    log_lb, log_1m_lb, one_m_lb, lb = lbc[0:1], lbc[1:2], lbc[2:3], lbc[3:4]
    zf = zf.astype(F32)
    e = jnp.exp2(jnp.abs(zf) * (-LOG2E))
    den = 1.0 + e
    inv = 1.0 / den
    t = one_m_lb * jnp.where(zf >= 0.0, inv, e * inv)
    fd = lb + t
    kk = one_m_lb - t
    x2 = log_1m_lb + (jnp.minimum(zf, 0.0) - jnp.log(den))
    lf2 = (jnp.maximum(log_lb, x2)
           + jnp.log(1.0 + jnp.exp2(jnp.abs(log_lb - x2) * (-LOG2E)))) * LOG2E
    q = _silu(zq.astype(F32))

    w3 = jnp.concatenate(
        [jnp.concatenate([p[c * chunk:(c + 1) * chunk] for c in range(nch)], axis=1)
         for p in _split3(lf2)], axis=0)
    xw = _dot(mc, w3)

    ps, qds, upds, decs = [], [], [], []
    for c in range(nch):
        sl = slice(c * chunk, (c + 1) * chunk)
        b2 = xw[:, c * dk:(c + 1) * dk]
        qc, kc = q[sl], kk[sl]
        qes, kes, halves = _hg_level_operands(qc, kc, fd[sl], b2, chunk)
        p = _dot_dims(qes[1].astype(BF16), kes[1].astype(BF16), NT_DIMS)
        for lv in [0] + list(range(2, len(halves))):
            s = _dot_dims(qes[lv].astype(BF16), kes[lv].astype(BF16), NT_DIMS)
            p = p + s * pm_ref[0 if lv == 0 else lv - 1]
        ps.append(p.astype(BF16))
        qds.append((qc * jnp.exp2(b2)).astype(BF16))
        b_last = b2[chunk - 1:chunk, :]
        k_dec = (kc * jnp.exp2(b_last - b2)).astype(BF16)
        upds.append(_dot_dims(v[sl], k_dec, TN_DIMS))
        decs.append(jnp.exp2(b_last))

    outs = []
    for c in range(nch):
        sl = slice(c * chunk, (c + 1) * chunk)
        outs.append(_dot(ps[c], v[sl]) + _dot_dims(qds[c], st.astype(BF16), NT_DIMS))
        st = st * decs[c] + upds[c]

    o = jnp.concatenate(outs, axis=0)
    o = o * lax.rsqrt(jnp.mean(o * o, axis=-1, keepdims=True) + NORM_EPS) * gain
    return o * _silu(zg.astype(F32)), st


def _hgrn2_kernel(q_ref, f_ref, i_ref, g_ref, lbc_ref, gain_ref, mc_ref, pm_ref, o_ref, st_sc, *, chunk, sub):
    tt, dk = q_ref.shape

    @pl.when(pl.program_id(2) == 0)
    def _():
        st_sc[...] = jnp.zeros(st_sc.shape, F32)

    def body(j, carry):
        rows = pl.ds(pl.multiple_of(j * sub, sub), sub)
        y, st = _hg_core(q_ref[rows, :], f_ref[rows, :], i_ref[rows, :], g_ref[rows, :], lbc_ref[0],
                         gain_ref[0], mc_ref[...], pm_ref, st_sc[...], chunk)
        st_sc[...] = st
        o_ref[rows, :] = y.astype(o_ref.dtype)
        return carry

    lax.fori_loop(0, tt // sub, body, 0)


def _hgrn2(z, lbc, gain, *, batch, seq, d_hg, col0, tt, sub, chunk):
    dk = HG_HEAD_DIM
    heads = d_hg // dk
    tt = min(tt, seq)
    sub = min(sub, tt)
    chunk = min(chunk, sub)
    assert chunk >= 2 * SUBLANES and tt % sub == 0 and sub % chunk == 0
    nt = seq // tt
    T = batch * seq
    mc = jnp.asarray(_hg_constants(chunk), BF16)
    pm = jnp.asarray(_hg_masks(chunk), F32)
    base = col0 // dk

    def zspec(j):
        return pl.BlockSpec((tt, dk), lambda b, h, t: (b * nt + t, base + j * heads + h))

    return pl.pallas_call(
        functools.partial(_hgrn2_kernel, chunk=chunk, sub=sub),
        grid=(batch, heads, nt),
        in_specs=[zspec(0), zspec(1), zspec(2), zspec(3),
                  pl.BlockSpec((1, 4, dk), lambda b, h, t: (h, 0, 0)),
                  pl.BlockSpec((1, 1, dk), lambda b, h, t: (h, 0, 0)),
                  pl.BlockSpec(mc.shape, lambda b, h, t: (0, 0)),
                  pl.BlockSpec(pm.shape, lambda b, h, t: (0, 0, 0))],
        out_specs=pl.BlockSpec((tt, dk), lambda b, h, t: (b * nt + t, h)),
        out_shape=jax.ShapeDtypeStruct((T, d_hg), BF16),
        scratch_shapes=[pltpu.VMEM((dk, dk), F32)],
        compiler_params=pltpu.CompilerParams(
            dimension_semantics=("parallel", "parallel", "arbitrary")),
        name="hgrn2",
    )(z, z, z, z, lbc, gain.reshape(heads, 1, dk), mc, pm)


def _lb_kernel(p_ref, o_ref):
    depth = p_ref.shape[0]
    p = p_ref[...]
    mx = jnp.max(p, axis=0, keepdims=True)
    ex = jnp.exp(p - mx)
    sm = ex / jnp.sum(ex, axis=0, keepdims=True)
    c0 = sm[0:1, :]
    c = c0
    for l in range(depth):
        if l > 0:
            c = c + sm[l:l + 1, :]
        lb = c - c0
        o_ref[l, 0:1, :] = jnp.log(lb)
        o_ref[l, 1:2, :] = jnp.log1p(-lb)
        o_ref[l, 2:3, :] = 1.0 - lb
        o_ref[l, 3:4, :] = lb


def _lb_constants(lb_param):
    depth, d_hg = lb_param.shape
    return pl.pallas_call(
        _lb_kernel,
        out_shape=jax.ShapeDtypeStruct((depth, 4, d_hg), F32),
        name="hgrn2_lower_bounds",
    )(lb_param)


def kernel(x, ffn1_norm, ffn1_w_gate, ffn1_w_up, ffn1_w_down, mix_norm, w_in, conv_w, conv_b, rg_wa, rg_ba, rg_wx, rg_bx, rg_lambda, hg_lower_bounds, hg_out_norm, w_out_a, w_out_b, w_out, ffn2_norm, ffn2_w_gate, ffn2_w_up, ffn2_w_down, final_norm):
    batch, seq, d_model = x.shape
    depth = w_in.shape[0]
    d_rg = rg_lambda.shape[1]
    d_hg = hg_lower_bounds.shape[1]
    heads_hg = d_hg // HG_HEAD_DIM
    T = batch * seq
    tm, tn = ROW_TILE, COL_TILE

    lbc = _lb_constants(hg_lower_bounds)
    lbc = lbc.reshape(depth, 4, heads_hg, HG_HEAD_DIM).transpose(0, 2, 1, 3)

    bf = lambda w: w.astype(BF16)
    names = ("wg1", "wu1", "wd1", "w_in", "wa", "wb", "w_out", "wg2", "wu2", "wd2")
    stacks = dict(zip(names, (ffn1_w_gate, ffn1_w_up, ffn1_w_down, w_in, w_out_a, w_out_b, w_out,
                              ffn2_w_gate, ffn2_w_up, ffn2_w_down)))
    wt = {(k, 0): bf(stacks[k][0]) for k in ("wg1", "wu1", "wd1")}
    rg_wa16, rg_wx16 = bf(rg_wa), bf(rg_wx)

    def run(fn, to_cast, *args, **kw):
        todo = [(k, ll) for k, ll in to_cast if ll < depth and (k, ll) not in wt]
        out, cast = fn(*args, [(stacks[k], ll) for k, ll in todo], **kw)
        wt.update(zip(todo, cast))
        return out

    h = x.reshape(T, d_model)
    for l in range(depth):
        early = [("w_in", l), ("wa", l), ("wb", l)]
        act = run(_norm_mm, [("wg1", l + 1), ("wu1", l + 1)] + early,
                  h, ffn1_norm[l], [wt["wg1", l], wt["wu1", l]], tm=tm // 2, tn=tn)
        early = [("w_out", l), ("wg2", l), ("wu2", l), ("wd2", l)]
        h = run(_mm_res, [("wd1", l + 1)] + early,
                act, wt["wd1", l], h, FFN_RES_WEIGHT, tm=tm, tn=tn, in_place=l > 0)

        z = run(_norm_mm, [("w_in", l + 1)], h, mix_norm[l], [wt["w_in", l]], tm=tm // 2, tn=2 * tn)
        y_a = _rglru(z, conv_w[l], conv_b[l], rg_wa16[l], rg_ba[l], rg_wx16[l], rg_bx[l], rg_lambda[l],
                     batch=batch, seq=seq, d_rg=d_rg, tt=SEQ_TILE)
        y_b = _hgrn2(z, lbc[l], hg_out_norm[l], batch=batch, seq=seq, d_hg=d_hg,
                     col0=2 * d_rg, tt=HG_SEQ_TILE, sub=HG_SUB, chunk=HG_CHUNK)
        ga_col = 2 * d_rg + 4 * d_hg
        merged = run(_merge, [("wa", l + 1), ("wb", l + 1)], y_a, y_b, wt["wa", l], wt["wb", l], z,
                     ga_col, ga_col + d_model, tm=tm, tn=tn)
        h = run(_mm_res, [("w_out", l + 1)], merged, wt["w_out", l], h, 1.0, tm=tm, tn=tn, in_place=True)

        act = run(_norm_mm, [("wg2", l + 1), ("wu2", l + 1)],
                  h, ffn2_norm[l], [wt["wg2", l], wt["wu2", l]], tm=tm // 2, tn=tn)
        h = run(_mm_res, [("wd2", l + 1)], act, wt["wd2", l], h, FFN_RES_WEIGHT, tm=tm, tn=tn, in_place=True)
    out = _rmsnorm(h, final_norm, tr=256)
    return out.reshape(batch, seq, d_model)
```

```python
import functools
import math

import numpy as np
import jax
import jax.numpy as jnp
from jax import lax
from jax.experimental import pallas as pl
from jax.experimental.pallas import tpu as pltpu

F32 = jnp.float32
BF16 = jnp.bfloat16

NORM_EPS = 1e-6
FFN_RES_WEIGHT = 0.5
RG_C = 8.0
HG_HEAD_DIM = 128

V7X_VMEM_BYTES = 64 * 1024 * 1024
SUBLANES = 8
LANES = 128
BF16_ROWS = 16

ROW_TILE = 1024
COL_TILE = 512
NORM_ROWS = 128
SEQ_TILE = 512
HG_SEQ_TILE = 1024
HG_SUB = 1024
HG_CHUNK = 64

LOG2E = 1.4426950408889634
NT_DIMS = (((1,), (1,)), ((), ()))
TN_DIMS = (((0,), (0,)), ((), ()))


def _vmem_limit(block_bytes):
    return int(min(V7X_VMEM_BYTES - 4 * 1024 * 1024, block_bytes * 1.3 + 8 * 1024 * 1024))


def _tile(n, pref):
    t = min(pref, n)
    while n % t:
        t //= 2
    return t


def _dot(a, b):
    return jnp.dot(a, b, preferred_element_type=F32)


def _dot_dims(a, b, dims):
    return lax.dot_general(a, b, dims, preferred_element_type=F32)


def _sigmoid(x):
    return 1.0 / (1.0 + jnp.exp2(x * (-LOG2E)))


def _silu(x):
    return x * _sigmoid(x)


def _zero_tile_after(e):
    rows, cols = e.shape
    r = e.reshape(rows // SUBLANES, SUBLANES, cols).sum(axis=0)
    r = functools.reduce(lambda a, b: a + b, [r[:, k:k + LANES] for k in range(0, cols, LANES)])
    bits = pltpu.bitcast(r, jnp.uint32)
    sixteen = jnp.uint32(16)
    z = pltpu.bitcast(lax.shift_right_logical(lax.shift_right_logical(bits, sixteen), sixteen), F32)
    return jnp.concatenate([z, z], axis=0).astype(BF16)


def _norm_rows(h_ref, g_ref, u_ref, rows):
    D = h_ref.shape[1]
    ssq = None
    for c in range(0, D, D // 2):
        x = h_ref[rows, c:c + D // 2]
        part = jnp.sum(x * x, axis=-1, keepdims=True)
        ssq = part if ssq is None else ssq + part
    scale = lax.rsqrt(ssq * (1.0 / D) + NORM_EPS)
    for c in range(0, D, D // 4):
        cols = slice(c, c + D // 4)
        u_ref[rows, cols] = (h_ref[rows, cols] * scale * g_ref[:, cols]).astype(BF16)


def _cast_rows(rows, steps):
    pr = BF16_ROWS
    while rows % pr or rows // pr > steps:
        pr += BF16_ROWS
    return pr


def _cast_plan(casts, steps, linear):
    in_specs, out_specs, out_shapes, operands = [], [], [], []
    for stack, layer in casts:
        _, R, C = stack.shape
        pr = _cast_rows(R, steps)
        pieces = R // pr

        def piece(*idx, pieces=pieces):
            return jnp.minimum(linear(*idx), pieces - 1)

        in_specs.append(pl.BlockSpec((None, pr, C), lambda *idx, layer=layer, piece=piece: (layer, piece(*idx), 0)))
        out_specs.append(pl.BlockSpec((pr, C), lambda *idx, piece=piece: (piece(*idx), 0)))
        out_shapes.append(jax.ShapeDtypeStruct((R, C), BF16))
        operands.append(stack)
    return in_specs, out_specs, out_shapes, operands


def _cast_bytes(casts, steps):
    total = 0
    for stack, _ in casts:
        _, R, C = stack.shape
        total += 2 * _cast_rows(R, steps) * C * (4 + 2)
    return total


def _do_casts(src_refs, dst_refs):
    for s, d in zip(src_refs, dst_refs):
        d[...] = s[...].astype(BF16)


def _split_refs(refs, n_in, n_cast):
    ins = refs[:n_in]
    cast_src = refs[n_in:n_in + n_cast]
    out = refs[n_in + n_cast]
    cast_dst = refs[n_in + n_cast + 1:n_in + 2 * n_cast + 1]
    scratch = refs[n_in + 2 * n_cast + 1:]
    return ins, cast_src, out, cast_dst, scratch


def _norm_mm_kernel(*refs, n_w, n_cast):
    (h_ref, g_ref, *w_refs), cast_src, o_ref, cast_dst, (u_sc,) = _split_refs(refs, 2 + n_w, n_cast)

    @pl.when(pl.program_id(1) == 0)
    def _():
        rows = h_ref.shape[0]
        step = min(NORM_ROWS, rows)

        def body(i, c):
            _norm_rows(h_ref, g_ref, u_sc, pl.ds(pl.multiple_of(i * step, step), step))
            return c

        lax.fori_loop(0, rows // step, body, 0)

    u = u_sc[...]
    if n_w == 2:
        g = _dot(u, w_refs[0][...])
        up = _dot(u, w_refs[1][...])
        o_ref[...] = (_silu(g) * up).astype(o_ref.dtype)
    else:
        o_ref[...] = _dot(u, w_refs[0][...]).astype(o_ref.dtype)
    _do_casts(cast_src, cast_dst)


def _norm_mm(h, gain, ws, casts, *, tm, tn):
    T, D = h.shape
    N = ws[0].shape[1]
    tm = _tile(T, tm)
    tn = _tile(N, tn)
    grid = (T // tm, N // tn)
    w_spec = pl.BlockSpec((D, tn), lambda m, n: (0, n))
    c_in, c_out, c_shapes, c_ops = _cast_plan(casts, grid[0] * grid[1], lambda m, n: m * grid[1] + n)
    est = (2 * tm * D * 4 + tm * D * 2 + len(ws) * 2 * D * tn * 2 + 2 * tm * tn * 2
           + (1 + len(ws)) * tm * tn * 4 + _cast_bytes(casts, grid[0] * grid[1]))
    outs = pl.pallas_call(
        functools.partial(_norm_mm_kernel, n_w=len(ws), n_cast=len(casts)),
        grid=grid,
        in_specs=[pl.BlockSpec((tm, D), lambda m, n: (m, 0)),
                  pl.BlockSpec((1, D), lambda m, n: (0, 0))] + [w_spec] * len(ws) + c_in,
        out_specs=[pl.BlockSpec((tm, tn), lambda m, n: (m, n))] + c_out,
        out_shape=[jax.ShapeDtypeStruct((T, N), BF16)] + c_shapes,
        scratch_shapes=[pltpu.VMEM((tm, D), BF16)],
        compiler_params=pltpu.CompilerParams(
            dimension_semantics=("arbitrary", "arbitrary"),
            vmem_limit_bytes=_vmem_limit(est)),
        name="norm_glu" if len(ws) == 2 else "norm_inproj",
    )(h, gain.reshape(1, D), *ws, *c_ops)
    return outs[0], outs[1:]


def _glu_kernel(*refs, n_cast, n_col, n_steps):
    (h_ref, g_ref, wg_ref, wu_ref), cast_src, o_ref, cast_dst, (u_sc, acc_sc) = _split_refs(refs, 4, n_cast)
    s = pl.program_id(0)
    last = n_steps - 1
    slot = s % 2

    @pl.when(jnp.logical_and(s < last, s % n_col == 0))
    def _():
        rows = h_ref.shape[0]
        step = min(NORM_ROWS, rows)

        def body(i, c):
            _norm_rows(h_ref, g_ref, u_sc, pl.ds(pl.multiple_of(i * step, step), step))
            return c

        lax.fori_loop(0, rows // step, body, 0)

    def dots(after=None):
        acc_sc[slot, 0] = _dot(u_sc[...], wg_ref[...])
        if after is not None:
            u_sc[0:BF16_ROWS, 0:LANES] = u_sc[0:BF16_ROWS, 0:LANES] + _zero_tile_after(after)
        acc_sc[slot, 1] = _dot(u_sc[...], wu_ref[...])

    def epilogue():
        tm = o_ref.shape[0]
        rsum = None
        for r0 in range(0, tm, 4 * BF16_ROWS):
            rows = slice(r0, r0 + 4 * BF16_ROWS)
            e = _silu(acc_sc[1 - slot, 0, rows, :]) * acc_sc[1 - slot, 1, rows, :]
            o_ref[rows, :] = e.astype(o_ref.dtype)
            part = e.reshape(-1, SUBLANES, e.shape[1]).sum(axis=0)
            rsum = part if rsum is None else rsum + part
        return rsum

    @pl.when(s == 0)
    def _():
        dots()

    @pl.when(jnp.logical_and(s > 0, s < last))
    def _():
        dots(after=epilogue())

    @pl.when(s == last)
    def _():
        epilogue()

    _do_casts(cast_src, cast_dst)


def _norm_glu(h, gain, wg, wu, casts, *, tm, tn):
    T, D = h.shape
    N = wg.shape[1]
    tm = _tile(T, tm)
    tn = _tile(N, tn)
    n_col = N // tn
    n_tiles = (T // tm) * n_col
    n_steps = n_tiles + 1
    cur = lambda s: jnp.minimum(s, n_tiles - 1)
    prev = lambda s: jnp.maximum(s - 1, 0)
    w_spec = pl.BlockSpec((D, tn), lambda s: (0, cur(s) % n_col))
    c_in, c_out, c_shapes, c_ops = _cast_plan(casts, n_steps, lambda s: s)
    est = (2 * tm * D * 4 + tm * D * 2 + 4 * D * tn * 2 + 2 * tm * tn * 2 + 6 * tm * tn * 4
           + _cast_bytes(casts, n_steps))
    outs = pl.pallas_call(
        functools.partial(_glu_kernel, n_cast=len(casts), n_col=n_col, n_steps=n_steps),
        grid=(n_steps,),
        in_specs=[pl.BlockSpec((tm, D), lambda s: (cur(s) // n_col, 0)),
                  pl.BlockSpec((1, D), lambda s: (0, 0)), w_spec, w_spec] + c_in,
        out_specs=[pl.BlockSpec((tm, tn), lambda s: (prev(s) // n_col, prev(s) % n_col))] + c_out,
        out_shape=[jax.ShapeDtypeStruct((T, N), BF16)] + c_shapes,
        scratch_shapes=[pltpu.VMEM((tm, D), BF16), pltpu.VMEM((2, 2, tm, tn), F32)],
        compiler_params=pltpu.CompilerParams(
            dimension_semantics=("arbitrary",),
            vmem_limit_bytes=_vmem_limit(est)),
        name="norm_glu",
    )(h, gain.reshape(1, D), wg, wu, *c_ops)
    return outs[0], outs[1:]


def _mm_res_kernel(*refs, scale, n_cast):
    (a_ref, w_ref, h_ref), cast_src, o_ref, cast_dst, _ = _split_refs(refs, 3, n_cast)
    acc = _dot(a_ref[...], w_ref[...])
    o_ref[...] = h_ref[...] + scale * acc
    _do_casts(cast_src, cast_dst)


def _mm_res(a, w, h, scale, casts, *, tm, tn, in_place):
    T, K = a.shape
    N = w.shape[1]
    tm = _tile(T, tm)
    tn = _tile(N, tn)
    grid = (T // tm, N // tn)
    c_in, c_out, c_shapes, c_ops = _cast_plan(casts, grid[0] * grid[1], lambda m, n: m * grid[1] + n)
    est = 2 * tm * K * 2 + 2 * K * tn * 2 + 4 * tm * tn * 4 + tm * tn * 4 + _cast_bytes(casts, grid[0] * grid[1])
    outs = pl.pallas_call(
        functools.partial(_mm_res_kernel, scale=scale, n_cast=len(casts)),
        grid=grid,
        in_specs=[pl.BlockSpec((tm, K), lambda m, n: (m, 0)),
                  pl.BlockSpec((K, tn), lambda m, n: (0, n)),
                  pl.BlockSpec((tm, tn), lambda m, n: (m, n))] + c_in,
        out_specs=[pl.BlockSpec((tm, tn), lambda m, n: (m, n))] + c_out,
        out_shape=[jax.ShapeDtypeStruct((T, N), F32)] + c_shapes,
        input_output_aliases={2: 0} if in_place else {},
        compiler_params=pltpu.CompilerParams(
            dimension_semantics=("arbitrary", "arbitrary"),
            vmem_limit_bytes=_vmem_limit(est)),
        name="mm_residual",
    )(a, w, h, *c_ops)
    return outs[0], outs[1:]


def _merge_kernel(*refs, n_cast):
    (ya_ref, yb_ref, wa_ref, wb_ref, ga_ref, gb_ref), cast_src, o_ref, cast_dst, _ = _split_refs(refs, 6, n_cast)
    a = _dot(ya_ref[...], wa_ref[...])
    b = _dot(yb_ref[...], wb_ref[...])
    ga = _sigmoid(ga_ref[...].astype(F32))
    gb = _sigmoid(gb_ref[...].astype(F32))
    o_ref[...] = (ga * a + gb * b).astype(o_ref.dtype)
    _do_casts(cast_src, cast_dst)


def _merge(ya, yb, wa, wb, z, ga_col, gb_col, casts, *, tm, tn):
    T, KA = ya.shape
    KB = yb.shape[1]
    N = wa.shape[1]
    tm = _tile(T, tm)
    tn = _tile(N, tn)
    grid = (T // tm, N // tn)
    ga_blk, gb_blk = ga_col // tn, gb_col // tn
    assert ga_blk * tn == ga_col and gb_blk * tn == gb_col
    c_in, c_out, c_shapes, c_ops = _cast_plan(casts, grid[0] * grid[1], lambda m, n: m * grid[1] + n)
    est = (2 * tm * (KA + KB) * 2 + 2 * (KA + KB) * tn * 2 + 6 * tm * tn * 2 + 4 * tm * tn * 4
           + _cast_bytes(casts, grid[0] * grid[1]))
    outs = pl.pallas_call(
        functools.partial(_merge_kernel, n_cast=len(casts)),
        grid=grid,
        in_specs=[pl.BlockSpec((tm, KA), lambda m, n: (m, 0)),
                  pl.BlockSpec((tm, KB), lambda m, n: (m, 0)),
                  pl.BlockSpec((KA, tn), lambda m, n: (0, n)),
                  pl.BlockSpec((KB, tn), lambda m, n: (0, n)),
                  pl.BlockSpec((tm, tn), lambda m, n: (m, ga_blk + n)),
                  pl.BlockSpec((tm, tn), lambda m, n: (m, gb_blk + n))] + c_in,
        out_specs=[pl.BlockSpec((tm, tn), lambda m, n: (m, n))] + c_out,
        out_shape=[jax.ShapeDtypeStruct((T, N), BF16)] + c_shapes,
        compiler_params=pltpu.CompilerParams(
            dimension_semantics=("arbitrary", "arbitrary"),
            vmem_limit_bytes=_vmem_limit(est)),
        name="gated_merge",
    )(ya, yb, wa, wb, z, z, *c_ops)
    return outs[0], outs[1:]


def _rmsnorm_kernel(h_ref, g_ref, o_ref):
    x = h_ref[...]
    ms = jnp.mean(x * x, axis=-1, keepdims=True)
    o_ref[...] = x * lax.rsqrt(ms + NORM_EPS) * g_ref[...]


def _rmsnorm(h, gain, *, tr):
    T, D = h.shape
    tr = min(tr, T)
    return pl.pallas_call(
        _rmsnorm_kernel,
        grid=(T // tr,),
        in_specs=[pl.BlockSpec((tr, D), lambda i: (i, 0)),
                  pl.BlockSpec((1, D), lambda i: (0, 0))],
        out_specs=pl.BlockSpec((tr, D), lambda i: (i, 0)),
        out_shape=jax.ShapeDtypeStruct((T, D), F32),
        compiler_params=pltpu.CompilerParams(dimension_semantics=("parallel",)),
        name="final_norm",
    )(h, gain.reshape(1, D))


def _rglru_kernel(x_ref, y_ref, cw_ref, cb_ref, wa_ref, ba_ref, wx_ref, bx_ref, lam_ref,
                  o_ref, xp_sc, carry_sc, *, conv_width):
    tt, cb = x_ref.shape
    pad = SUBLANES

    @pl.when(pl.program_id(2) == 0)
    def _():
        xp_sc[...] = jnp.zeros((pad, cb), F32)
        carry_sc[...] = jnp.zeros((SUBLANES, cb), F32)

    x = x_ref[...].astype(F32)
    xpad = jnp.concatenate([xp_sc[...], x], axis=0)
    cw = cw_ref[0]
    xc = cb_ref[0] + x * cw[conv_width - 1:conv_width, :]
    for s in range(1, conv_width):
        xc = xc + pltpu.roll(xpad, s, 0)[pad:] * cw[conv_width - 1 - s:conv_width - s, :]
    xp_sc[...] = x[tt - pad:tt]

    xcb = xc.astype(BF16)
    r = _sigmoid(_dot(xcb, wa_ref[0]) + ba_ref[0])
    i = _sigmoid(_dot(xcb, wx_ref[0]) + bx_ref[0])
    log_a = (RG_C * jax.nn.log_sigmoid(lam_ref[0])) * r
    a = jnp.exp(log_a)
    b = jnp.sqrt(-jnp.tanh(log_a) * (a * a + 1.0)) * (i * xc)

    a = a.reshape(tt // SUBLANES, SUBLANES, cb)
    b = b.reshape(tt // SUBLANES, SUBLANES, cb)
    row8 = lax.broadcasted_iota(jnp.int32, a.shape, 1)
    for d in (1, 2, 4):
        keep = row8 >= d
        a_prev = jnp.where(keep, pltpu.roll(a, d, 1), 1.0)
        b_prev = jnp.where(keep, pltpu.roll(b, d, 1), 0.0)
        b = a * b_prev + b
        a = a * a_prev
    a = a.reshape(tt, cb)
    b = b.reshape(tt, cb)
    carry = carry_sc[...]
    hs = []
    for gi in range(tt // SUBLANES):
        g8 = slice(gi * SUBLANES, (gi + 1) * SUBLANES)
        hg = a[g8] * carry + b[g8]
        hs.append(hg)
        carry = jnp.broadcast_to(hg[SUBLANES - 1:SUBLANES, :], (SUBLANES, cb))
    carry_sc[...] = carry
    hseq = jnp.concatenate(hs, axis=0)
    o_ref[...] = (hseq * jax.nn.gelu(y_ref[...].astype(F32))).astype(o_ref.dtype)


def _rglru(z, cw, cb, wa, ba, wx, bx, lam, *, batch, seq, d_rg, tt):
    heads, blk = wa.shape[0], wa.shape[1]
    conv_width = cw.shape[0]
    tt = min(tt, seq)
    nt = seq // tt
    T = batch * seq
    vec = lambda v: v.reshape(heads, 1, blk)
    vspec = pl.BlockSpec((1, 1, blk), lambda b, h, t: (h, 0, 0))
    wspec = pl.BlockSpec((1, blk, blk), lambda b, h, t: (h, 0, 0))
    cwh = cw.reshape(conv_width, heads, blk).transpose(1, 0, 2)
    return pl.pallas_call(
        functools.partial(_rglru_kernel, conv_width=conv_width),
        grid=(batch, heads, nt),
        in_specs=[pl.BlockSpec((tt, blk), lambda b, h, t: (b * nt + t, h)),
                  pl.BlockSpec((tt, blk), lambda b, h, t: (b * nt + t, heads + h)),
                  pl.BlockSpec((1, conv_width, blk), lambda b, h, t: (h, 0, 0)),
                  vspec, wspec, vspec, wspec, vspec, vspec],
        out_specs=pl.BlockSpec((tt, blk), lambda b, h, t: (b * nt + t, h)),
        out_shape=jax.ShapeDtypeStruct((T, d_rg), BF16),
        scratch_shapes=[pltpu.VMEM((SUBLANES, blk), F32),
                        pltpu.VMEM((SUBLANES, blk), F32)],
        compiler_params=pltpu.CompilerParams(
            dimension_semantics=("parallel", "parallel", "arbitrary")),
        name="rglru",
    )(z, z, cwh, vec(cb), wa, vec(ba), wx, vec(bx), vec(lam))


def _hg_constants(chunk):
    t = np.arange(chunk)[:, None]
    r = np.arange(chunk)[None, :]
    tril = (r <= t).astype(np.float32)
    return np.concatenate([tril, tril, tril], axis=1)


def _hg_masks(chunk):
    t = np.arange(chunk)[:, None]
    s = np.arange(chunk)[None, :]
    masks = [(t == s)]
    m = chunk // 4
    while m >= 1:
        masks.append(((t ^ s) >> int(math.log2(m))) == 1)
        m //= 2
    return np.stack(masks).astype(np.float32)


def _split3(x):
    hi = x.astype(BF16)
    r1 = x - hi.astype(F32)
    mid = r1.astype(BF16)
    lo = (r1 - mid.astype(F32)).astype(BF16)
    return hi, mid, lo


def _hg_level_operands(q, kk, fd, b2, chunk):
    dk = q.shape[1]
    zeros = lambda n: jnp.zeros((n, dk), F32)
    qes, kes, halves = [q], [kk], [0]
    m = chunk // 2
    while m >= SUBLANES:
        qparts, kparts = [], []
        for s in range(0, chunk, 2 * m):
            ref = b2[s + m - 1:s + m, :]
            qparts += [zeros(m), q[s + m:s + 2 * m] * jnp.exp2(b2[s + m:s + 2 * m] - ref)]
            kparts += [kk[s:s + m] * jnp.exp2(ref - b2[s:s + m]), zeros(m)]
        qes.append(jnp.concatenate(qparts, axis=0))
        kes.append(jnp.concatenate(kparts, axis=0))
        halves.append(m)
        m //= 2
    row = lax.broadcasted_iota(jnp.int32, (chunk, dk), 0)
    b3 = b2.reshape(chunk // SUBLANES, SUBLANES, dk)
    ref = jnp.broadcast_to(b3[:, 3:4, :], b3.shape).reshape(chunk, dk)
    w = jnp.exp2(-jnp.abs(b2 - ref))
    upper = (row & 4) != 0
    qes.append(jnp.where(upper, q * w, 0.0))
    kes.append(jnp.where(upper, 0.0, kk * w))
    halves.append(4)
    f3 = fd.reshape(chunk // SUBLANES, SUBLANES, dk)
    f_prev = pltpu.roll(f3, 1, 1).reshape(chunk, dk)
    f_next = pltpu.roll(f3, SUBLANES - 1, 1).reshape(chunk, dk)
    r4 = row & 3
    qes.append(jnp.where(r4 >= 2, q * jnp.where(r4 == 3, fd * f_prev, fd), 0.0))
    kes.append(jnp.where(r4 >= 2, 0.0, kk * jnp.where(r4 == 0, f_next, 1.0)))
    halves.append(2)
    odd = (row & 1) != 0
    qes.append(jnp.where(odd, q * fd, 0.0))
    kes.append(jnp.where(odd, 0.0, kk))
    halves.append(1)
    return qes, kes, halves


def _hg_core(zq, zf, v, zg, lbc, gain, mc, pm_ref, st, chunk):
    rows, dk = zq.shape
    nch = rows // chunk
    log_lb, log_1m_lb, one_m_lb, lb = lbc[0:1], lbc[1:2], lbc[2:3], lbc[3:4]
    zf = zf.astype(F32)
    e = jnp.exp2(jnp.abs(zf) * (-LOG2E))
    den = 1.0 + e
    inv = 1.0 / den
    t = one_m_lb * jnp.where(zf >= 0.0, inv, e * inv)
    fd = lb + t
    kk = one_m_lb - t
    x2 = log_1m_lb + (jnp.minimum(zf, 0.0) - jnp.log(den))
    lf2 = (jnp.maximum(log_lb, x2)
           + jnp.log(1.0 + jnp.exp2(jnp.abs(log_lb - x2) * (-LOG2E)))) * LOG2E
    q = _silu(zq.astype(F32))

    w3 = jnp.concatenate(
        [jnp.concatenate([p[c * chunk:(c + 1) * chunk] for c in range(nch)], axis=1)
         for p in _split3(lf2)], axis=0)
    xw = _dot(mc, w3)

    ps, qds, upds, decs = [], [], [], []
    for c in range(nch):
        sl = slice(c * chunk, (c + 1) * chunk)
        b2 = xw[:, c * dk:(c + 1) * dk]
        qc, kc = q[sl], kk[sl]
        qes, kes, halves = _hg_level_operands(qc, kc, fd[sl], b2, chunk)
        p = _dot_dims(qes[1].astype(BF16), kes[1].astype(BF16), NT_DIMS)
        for lv in [0] + list(range(2, len(halves))):
            s = _dot_dims(qes[lv].astype(BF16), kes[lv].astype(BF16), NT_DIMS)
            p = p + s * pm_ref[0 if lv == 0 else lv - 1]
        ps.append(p.astype(BF16))
        qds.append((qc * jnp.exp2(b2)).astype(BF16))
        b_last = b2[chunk - 1:chunk, :]
        k_dec = (kc * jnp.exp2(b_last - b2)).astype(BF16)
        upds.append(_dot_dims(v[sl], k_dec, TN_DIMS))
        decs.append(jnp.exp2(b_last))

    outs = []
    for c in range(nch):
        sl = slice(c * chunk, (c + 1) * chunk)
        outs.append(_dot(ps[c], v[sl]) + _dot_dims(qds[c], st.astype(BF16), NT_DIMS))
        st = st * decs[c] + upds[c]

    o = jnp.concatenate(outs, axis=0)
    o = o * lax.rsqrt(jnp.mean(o * o, axis=-1, keepdims=True) + NORM_EPS) * gain
    return o * _silu(zg.astype(F32)), st


def _hgrn2_kernel(q_ref, f_ref, i_ref, g_ref, lbc_ref, gain_ref, mc_ref, pm_ref, o_ref, st_sc, *, chunk, sub):
    tt, dk = q_ref.shape

    @pl.when(pl.program_id(2) == 0)
    def _():
        st_sc[...] = jnp.zeros(st_sc.shape, F32)

    def body(j, carry):
        rows = pl.ds(pl.multiple_of(j * sub, sub), sub)
        y, st = _hg_core(q_ref[rows, :], f_ref[rows, :], i_ref[rows, :], g_ref[rows, :], lbc_ref[0],
                         gain_ref[0], mc_ref[...], pm_ref, st_sc[...], chunk)
        st_sc[...] = st
        o_ref[rows, :] = y.astype(o_ref.dtype)
        return carry

    lax.fori_loop(0, tt // sub, body, 0)


def _hgrn2(z, lbc, gain, *, batch, seq, d_hg, col0, tt, sub, chunk):
    dk = HG_HEAD_DIM
    heads = d_hg // dk
    tt = min(tt, seq)
    sub = min(sub, tt)
    chunk = min(chunk, sub)
    assert chunk >= 2 * SUBLANES and tt % sub == 0 and sub % chunk == 0
    nt = seq // tt
    T = batch * seq
    mc = jnp.asarray(_hg_constants(chunk), BF16)
    pm = jnp.asarray(_hg_masks(chunk), F32)
    base = col0 // dk

    def zspec(j):
        return pl.BlockSpec((tt, dk), lambda b, h, t: (b * nt + t, base + j * heads + h))

    return pl.pallas_call(
        functools.partial(_hgrn2_kernel, chunk=chunk, sub=sub),
        grid=(batch, heads, nt),
        in_specs=[zspec(0), zspec(1), zspec(2), zspec(3),
                  pl.BlockSpec((1, 4, dk), lambda b, h, t: (h, 0, 0)),
                  pl.BlockSpec((1, 1, dk), lambda b, h, t: (h, 0, 0)),
                  pl.BlockSpec(mc.shape, lambda b, h, t: (0, 0)),
                  pl.BlockSpec(pm.shape, lambda b, h, t: (0, 0, 0))],
        out_specs=pl.BlockSpec((tt, dk), lambda b, h, t: (b * nt + t, h)),
        out_shape=jax.ShapeDtypeStruct((T, d_hg), BF16),
        scratch_shapes=[pltpu.VMEM((dk, dk), F32)],
        compiler_params=pltpu.CompilerParams(
            dimension_semantics=("parallel", "parallel", "arbitrary")),
        name="hgrn2",
    )(z, z, z, z, lbc, gain.reshape(heads, 1, dk), mc, pm)


def _lb_kernel(p_ref, o_ref):
    depth = p_ref.shape[0]
    p = p_ref[...]
    mx = jnp.max(p, axis=0, keepdims=True)
    ex = jnp.exp(p - mx)
    sm = ex / jnp.sum(ex, axis=0, keepdims=True)
    c0 = sm[0:1, :]
    c = c0
    for l in range(depth):
        if l > 0:
            c = c + sm[l:l + 1, :]
        lb = c - c0
        o_ref[l, 0:1, :] = jnp.log(lb)
        o_ref[l, 1:2, :] = jnp.log1p(-lb)
        o_ref[l, 2:3, :] = 1.0 - lb
        o_ref[l, 3:4, :] = lb


def _lb_constants(lb_param):
    depth, d_hg = lb_param.shape
    return pl.pallas_call(
        _lb_kernel,
        out_shape=jax.ShapeDtypeStruct((depth, 4, d_hg), F32),
        name="hgrn2_lower_bounds",
    )(lb_param)


def kernel(x, ffn1_norm, ffn1_w_gate, ffn1_w_up, ffn1_w_down, mix_norm, w_in, conv_w, conv_b, rg_wa, rg_ba, rg_wx, rg_bx, rg_lambda, hg_lower_bounds, hg_out_norm, w_out_a, w_out_b, w_out, ffn2_norm, ffn2_w_gate, ffn2_w_up, ffn2_w_down, final_norm):
    batch, seq, d_model = x.shape
    depth = w_in.shape[0]
    d_rg = rg_lambda.shape[1]
    d_hg = hg_lower_bounds.shape[1]
    heads_hg = d_hg // HG_HEAD_DIM
    T = batch * seq
    tm, tn = ROW_TILE, COL_TILE

    lbc = _lb_constants(hg_lower_bounds)
    lbc = lbc.reshape(depth, 4, heads_hg, HG_HEAD_DIM).transpose(0, 2, 1, 3)

    bf = lambda w: w.astype(BF16)
    names = ("wg1", "wu1", "wd1", "w_in", "wa", "wb", "w_out", "wg2", "wu2", "wd2")
    stacks = dict(zip(names, (ffn1_w_gate, ffn1_w_up, ffn1_w_down, w_in, w_out_a, w_out_b, w_out,
                              ffn2_w_gate, ffn2_w_up, ffn2_w_down)))
    wt = {(k, 0): bf(stacks[k][0]) for k in ("wg1", "wu1", "wd1")}
    rg_wa16, rg_wx16 = bf(rg_wa), bf(rg_wx)

    def run(fn, to_cast, *args, **kw):
        todo = [(k, ll) for k, ll in to_cast if ll < depth and (k, ll) not in wt]
        out, cast = fn(*args, [(stacks[k], ll) for k, ll in todo], **kw)
        wt.update(zip(todo, cast))
        return out

    h = x.reshape(T, d_model)
    for l in range(depth):
        early = [("w_in", l), ("wa", l), ("wb", l)]
        act = run(_norm_glu, [("wg1", l + 1), ("wu1", l + 1)] + early,
                  h, ffn1_norm[l], wt["wg1", l], wt["wu1", l], tm=tm // 2, tn=tn)
        early = [("w_out", l), ("wg2", l), ("wu2", l), ("wd2", l)]
        h = run(_mm_res, [("wd1", l + 1)] + early,
                act, wt["wd1", l], h, FFN_RES_WEIGHT, tm=tm, tn=tn, in_place=l > 0)

        z = run(_norm_mm, [("w_in", l + 1)], h, mix_norm[l], [wt["w_in", l]], tm=tm // 2, tn=2 * tn)
        y_a = _rglru(z, conv_w[l], conv_b[l], rg_wa16[l], rg_ba[l], rg_wx16[l], rg_bx[l], rg_lambda[l],
                     batch=batch, seq=seq, d_rg=d_rg, tt=SEQ_TILE)
        y_b = _hgrn2(z, lbc[l], hg_out_norm[l], batch=batch, seq=seq, d_hg=d_hg,
                     col0=2 * d_rg, tt=HG_SEQ_TILE, sub=HG_SUB, chunk=HG_CHUNK)
        ga_col = 2 * d_rg + 4 * d_hg
        merged = run(_merge, [("wa", l + 1), ("wb", l + 1)], y_a, y_b, wt["wa", l], wt["wb", l], z,
                     ga_col, ga_col + d_model, tm=tm, tn=tn)
        h = run(_mm_res, [("w_out", l + 1)], merged, wt["w_out", l], h, 1.0, tm=tm, tn=tn, in_place=True)

        act = run(_norm_glu, [("wg2", l + 1), ("wu2", l + 1)],
                  h, ffn2_norm[l], wt["wg2", l], wt["wu2", l], tm=tm // 2, tn=tn)
        h = run(_mm_res, [("wd2", l + 1)], act, wt["wd2", l], h, FFN_RES_WEIGHT, tm=tm, tn=tn, in_place=True)
    out = _rmsnorm(h, final_norm, tr=256)
    return out.reshape(batch, seq, d_model)
```

```python
import functools
import math

import numpy as np
import jax
import jax.numpy as jnp
from jax import lax
from jax.experimental import pallas as pl
from jax.experimental.pallas import tpu as pltpu

F32 = jnp.float32
BF16 = jnp.bfloat16

NORM_EPS = 1e-6
FFN_RES_WEIGHT = 0.5
RG_C = 8.0
HG_HEAD_DIM = 128

V7X_VMEM_BYTES = 64 * 1024 * 1024
SUBLANES = 8
LANES = 128
BF16_ROWS = 16

ROW_TILE = 1024
COL_TILE = 512
NORM_ROWS = 128
SEQ_TILE = 512
HG_SEQ_TILE = 1024
HG_SUB = 512
HG_CHUNK = 64
HG_HEADS_PER_STEP = 4

LOG2E = 1.4426950408889634
NT_DIMS = (((1,), (1,)), ((), ()))
TN_DIMS = (((0,), (0,)), ((), ()))


def _vmem_limit(block_bytes):
    return int(min(V7X_VMEM_BYTES - 4 * 1024 * 1024, block_bytes * 1.3 + 8 * 1024 * 1024))


def _tile(n, pref):
    t = min(pref, n)
    while n % t:
        t //= 2
    return t


def _dot(a, b):
    return jnp.dot(a, b, preferred_element_type=F32)


def _dot_dims(a, b, dims):
    return lax.dot_general(a, b, dims, preferred_element_type=F32)


def _sigmoid(x):
    return 1.0 / (1.0 + jnp.exp2(x * (-LOG2E)))


def _silu(x):
    return x * _sigmoid(x)


def _norm_rows(h_ref, g_ref, u_ref, rows):
    D = h_ref.shape[1]
    ssq = None
    for c in range(0, D, D // 2):
        x = h_ref[rows, c:c + D // 2]
        part = jnp.sum(x * x, axis=-1, keepdims=True)
        ssq = part if ssq is None else ssq + part
    scale = lax.rsqrt(ssq * (1.0 / D) + NORM_EPS)
    for c in range(0, D, D // 4):
        cols = slice(c, c + D // 4)
        u_ref[rows, cols] = (h_ref[rows, cols] * scale * g_ref[:, cols]).astype(BF16)


def _cast_rows(rows, steps):
    pr = BF16_ROWS
    while rows % pr or rows // pr > steps:
        pr += BF16_ROWS
    return pr


def _cast_plan(casts, grid):
    steps = grid[0] * grid[1]
    in_specs, out_specs, out_shapes, operands = [], [], [], []
    for stack, layer in casts:
        _, R, C = stack.shape
        pr = _cast_rows(R, steps)
        pieces = R // pr

        def piece(m, n, pieces=pieces):
            return jnp.minimum(m * grid[1] + n, pieces - 1)

        in_specs.append(pl.BlockSpec((None, pr, C), lambda m, n, layer=layer, piece=piece: (layer, piece(m, n), 0)))
        out_specs.append(pl.BlockSpec((pr, C), lambda m, n, piece=piece: (piece(m, n), 0)))
        out_shapes.append(jax.ShapeDtypeStruct((R, C), BF16))
        operands.append(stack)
    return in_specs, out_specs, out_shapes, operands


def _cast_bytes(casts, grid):
    steps = grid[0] * grid[1]
    total = 0
    for stack, _ in casts:
        _, R, C = stack.shape
        total += 2 * _cast_rows(R, steps) * C * (4 + 2)
    return total


def _do_casts(src_refs, dst_refs):
    for s, d in zip(src_refs, dst_refs):
        d[...] = s[...].astype(BF16)


def _split_refs(refs, n_in, n_cast):
    ins = refs[:n_in]
    cast_src = refs[n_in:n_in + n_cast]
    out = refs[n_in + n_cast]
    cast_dst = refs[n_in + n_cast + 1:n_in + 2 * n_cast + 1]
    scratch = refs[n_in + 2 * n_cast + 1:]
    return ins, cast_src, out, cast_dst, scratch


def _norm_mm_kernel(*refs, n_w, n_cast):
    (h_ref, g_ref, *w_refs), cast_src, o_ref, cast_dst, (u_sc,) = _split_refs(refs, 2 + n_w, n_cast)

    @pl.when(pl.program_id(1) == 0)
    def _():
        rows = h_ref.shape[0]
        step = min(NORM_ROWS, rows)

        def body(i, c):
            _norm_rows(h_ref, g_ref, u_sc, pl.ds(pl.multiple_of(i * step, step), step))
            return c

        lax.fori_loop(0, rows // step, body, 0)

    u = u_sc[...]
    if n_w == 2:
        g = _dot(u, w_refs[0][...])
        up = _dot(u, w_refs[1][...])
        o_ref[...] = (_silu(g) * up).astype(o_ref.dtype)
    else:
        o_ref[...] = _dot(u, w_refs[0][...]).astype(o_ref.dtype)
    _do_casts(cast_src, cast_dst)


def _norm_mm(h, gain, ws, casts, *, tm, tn):
    T, D = h.shape
    N = ws[0].shape[1]
    tm = _tile(T, tm)
    tn = _tile(N, tn)
    grid = (T // tm, N // tn)
    w_spec = pl.BlockSpec((D, tn), lambda m, n: (0, n))
    c_in, c_out, c_shapes, c_ops = _cast_plan(casts, grid)
    est = (2 * tm * D * 4 + tm * D * 2 + len(ws) * 2 * D * tn * 2 + 2 * tm * tn * 2
           + (1 + len(ws)) * tm * tn * 4 + _cast_bytes(casts, grid))
    outs = pl.pallas_call(
        functools.partial(_norm_mm_kernel, n_w=len(ws), n_cast=len(casts)),
        grid=grid,
        in_specs=[pl.BlockSpec((tm, D), lambda m, n: (m, 0)),
                  pl.BlockSpec((1, D), lambda m, n: (0, 0))] + [w_spec] * len(ws) + c_in,
        out_specs=[pl.BlockSpec((tm, tn), lambda m, n: (m, n))] + c_out,
        out_shape=[jax.ShapeDtypeStruct((T, N), BF16)] + c_shapes,
        scratch_shapes=[pltpu.VMEM((tm, D), BF16)],
        compiler_params=pltpu.CompilerParams(
            dimension_semantics=("arbitrary", "arbitrary"),
            vmem_limit_bytes=_vmem_limit(est)),
        name="norm_glu" if len(ws) == 2 else "norm_inproj",
    )(h, gain.reshape(1, D), *ws, *c_ops)
    return outs[0], outs[1:]


def _mm_res_kernel(*refs, scale, n_cast):
    (a_ref, w_ref, h_ref), cast_src, o_ref, cast_dst, _ = _split_refs(refs, 3, n_cast)
    acc = _dot(a_ref[...], w_ref[...])
    o_ref[...] = h_ref[...] + scale * acc
    _do_casts(cast_src, cast_dst)


def _mm_res(a, w, h, scale, casts, *, tm, tn, in_place):
    T, K = a.shape
    N = w.shape[1]
    tm = _tile(T, tm)
    tn = _tile(N, tn)
    grid = (T // tm, N // tn)
    c_in, c_out, c_shapes, c_ops = _cast_plan(casts, grid)
    est = 2 * tm * K * 2 + 2 * K * tn * 2 + 4 * tm * tn * 4 + tm * tn * 4 + _cast_bytes(casts, grid)
    outs = pl.pallas_call(
        functools.partial(_mm_res_kernel, scale=scale, n_cast=len(casts)),
        grid=grid,
        in_specs=[pl.BlockSpec((tm, K), lambda m, n: (m, 0)),
                  pl.BlockSpec((K, tn), lambda m, n: (0, n)),
                  pl.BlockSpec((tm, tn), lambda m, n: (m, n))] + c_in,
        out_specs=[pl.BlockSpec((tm, tn), lambda m, n: (m, n))] + c_out,
        out_shape=[jax.ShapeDtypeStruct((T, N), F32)] + c_shapes,
        input_output_aliases={2: 0} if in_place else {},
        compiler_params=pltpu.CompilerParams(
            dimension_semantics=("arbitrary", "arbitrary"),
            vmem_limit_bytes=_vmem_limit(est)),
        name="mm_residual",
    )(a, w, h, *c_ops)
    return outs[0], outs[1:]


def _merge_kernel(*refs, n_cast):
    (ya_ref, yb_ref, wa_ref, wb_ref, ga_ref, gb_ref), cast_src, o_ref, cast_dst, _ = _split_refs(refs, 6, n_cast)
    a = _dot(ya_ref[...], wa_ref[...])
    b = _dot(yb_ref[...], wb_ref[...])
    ga = _sigmoid(ga_ref[...].astype(F32))
    gb = _sigmoid(gb_ref[...].astype(F32))
    o_ref[...] = (ga * a + gb * b).astype(o_ref.dtype)
    _do_casts(cast_src, cast_dst)


def _merge(ya, yb, wa, wb, z, ga_col, gb_col, casts, *, tm, tn):
    T, KA = ya.shape
    KB = yb.shape[1]
    N = wa.shape[1]
    tm = _tile(T, tm)
    tn = _tile(N, tn)
    grid = (T // tm, N // tn)
    ga_blk, gb_blk = ga_col // tn, gb_col // tn
    assert ga_blk * tn == ga_col and gb_blk * tn == gb_col
    c_in, c_out, c_shapes, c_ops = _cast_plan(casts, grid)
    est = (2 * tm * (KA + KB) * 2 + 2 * (KA + KB) * tn * 2 + 6 * tm * tn * 2 + 4 * tm * tn * 4
           + _cast_bytes(casts, grid))
    outs = pl.pallas_call(
        functools.partial(_merge_kernel, n_cast=len(casts)),
        grid=grid,
        in_specs=[pl.BlockSpec((tm, KA), lambda m, n: (m, 0)),
                  pl.BlockSpec((tm, KB), lambda m, n: (m, 0)),
                  pl.BlockSpec((KA, tn), lambda m, n: (0, n)),
                  pl.BlockSpec((KB, tn), lambda m, n: (0, n)),
                  pl.BlockSpec((tm, tn), lambda m, n: (m, ga_blk + n)),
                  pl.BlockSpec((tm, tn), lambda m, n: (m, gb_blk + n))] + c_in,
        out_specs=[pl.BlockSpec((tm, tn), lambda m, n: (m, n))] + c_out,
        out_shape=[jax.ShapeDtypeStruct((T, N), BF16)] + c_shapes,
        compiler_params=pltpu.CompilerParams(
            dimension_semantics=("arbitrary", "arbitrary"),
            vmem_limit_bytes=_vmem_limit(est)),
        name="gated_merge",
    )(ya, yb, wa, wb, z, z, *c_ops)
    return outs[0], outs[1:]


def _rmsnorm_kernel(h_ref, g_ref, o_ref):
    x = h_ref[...]
    ms = jnp.mean(x * x, axis=-1, keepdims=True)
    o_ref[...] = x * lax.rsqrt(ms + NORM_EPS) * g_ref[...]


def _rmsnorm(h, gain, *, tr):
    T, D = h.shape
    tr = min(tr, T)
    return pl.pallas_call(
        _rmsnorm_kernel,
        grid=(T // tr,),
        in_specs=[pl.BlockSpec((tr, D), lambda i: (i, 0)),
                  pl.BlockSpec((1, D), lambda i: (0, 0))],
        out_specs=pl.BlockSpec((tr, D), lambda i: (i, 0)),
        out_shape=jax.ShapeDtypeStruct((T, D), F32),
        compiler_params=pltpu.CompilerParams(dimension_semantics=("parallel",)),
        name="final_norm",
    )(h, gain.reshape(1, D))


def _rglru_kernel(x_ref, y_ref, cw_ref, cb_ref, wa_ref, ba_ref, wx_ref, bx_ref, lam_ref,
                  o_ref, xp_sc, carry_sc, *, conv_width):
    tt, cb = x_ref.shape
    pad = SUBLANES

    @pl.when(pl.program_id(2) == 0)
    def _():
        xp_sc[...] = jnp.zeros((pad, cb), F32)
        carry_sc[...] = jnp.zeros((SUBLANES, cb), F32)

    x = x_ref[...].astype(F32)
    xpad = jnp.concatenate([xp_sc[...], x], axis=0)
    cw = cw_ref[0]
    xc = cb_ref[0] + x * cw[conv_width - 1:conv_width, :]
    for s in range(1, conv_width):
        xc = xc + pltpu.roll(xpad, s, 0)[pad:] * cw[conv_width - 1 - s:conv_width - s, :]
    xp_sc[...] = x[tt - pad:tt]

    xcb = xc.astype(BF16)
    r = _sigmoid(_dot(xcb, wa_ref[0]) + ba_ref[0])
    i = _sigmoid(_dot(xcb, wx_ref[0]) + bx_ref[0])
    log_a = (RG_C * jax.nn.log_sigmoid(lam_ref[0])) * r
    a = jnp.exp(log_a)
    b = jnp.sqrt(-jnp.tanh(log_a) * (a * a + 1.0)) * (i * xc)

    a = a.reshape(tt // SUBLANES, SUBLANES, cb)
    b = b.reshape(tt // SUBLANES, SUBLANES, cb)
    row8 = lax.broadcasted_iota(jnp.int32, a.shape, 1)
    for d in (1, 2, 4):
        keep = row8 >= d
        a_prev = jnp.where(keep, pltpu.roll(a, d, 1), 1.0)
        b_prev = jnp.where(keep, pltpu.roll(b, d, 1), 0.0)
        b = a * b_prev + b
        a = a * a_prev
    a = a.reshape(tt, cb)
    b = b.reshape(tt, cb)
    carry = carry_sc[...]
    hs = []
    for gi in range(tt // SUBLANES):
        g8 = slice(gi * SUBLANES, (gi + 1) * SUBLANES)
        hg = a[g8] * carry + b[g8]
        hs.append(hg)
        carry = jnp.broadcast_to(hg[SUBLANES - 1:SUBLANES, :], (SUBLANES, cb))
    carry_sc[...] = carry
    hseq = jnp.concatenate(hs, axis=0)
    o_ref[...] = (hseq * jax.nn.gelu(y_ref[...].astype(F32))).astype(o_ref.dtype)


def _rglru(z, cw, cb, wa, ba, wx, bx, lam, *, batch, seq, d_rg, tt):
    heads, blk = wa.shape[0], wa.shape[1]
    conv_width = cw.shape[0]
    tt = min(tt, seq)
    nt = seq // tt
    T = batch * seq
    vec = lambda v: v.reshape(heads, 1, blk)
    vspec = pl.BlockSpec((1, 1, blk), lambda b, h, t: (h, 0, 0))
    wspec = pl.BlockSpec((1, blk, blk), lambda b, h, t: (h, 0, 0))
    cwh = cw.reshape(conv_width, heads, blk).transpose(1, 0, 2)
    return pl.pallas_call(
        functools.partial(_rglru_kernel, conv_width=conv_width),
        grid=(batch, heads, nt),
        in_specs=[pl.BlockSpec((tt, blk), lambda b, h, t: (b * nt + t, h)),
                  pl.BlockSpec((tt, blk), lambda b, h, t: (b * nt + t, heads + h)),
                  pl.BlockSpec((1, conv_width, blk), lambda b, h, t: (h, 0, 0)),
                  vspec, wspec, vspec, wspec, vspec, vspec],
        out_specs=pl.BlockSpec((tt, blk), lambda b, h, t: (b * nt + t, h)),
        out_shape=jax.ShapeDtypeStruct((T, d_rg), BF16),
        scratch_shapes=[pltpu.VMEM((SUBLANES, blk), F32),
                        pltpu.VMEM((SUBLANES, blk), F32)],
        compiler_params=pltpu.CompilerParams(
            dimension_semantics=("parallel", "parallel", "arbitrary")),
        name="rglru",
    )(z, z, cwh, vec(cb), wa, vec(ba), wx, vec(bx), vec(lam))


def _hg_constants(chunk):
    t = np.arange(chunk)[:, None]
    r = np.arange(chunk)[None, :]
    tril = (r <= t).astype(np.float32)
    return np.concatenate([tril, tril, tril], axis=1)


def _hg_masks(chunk):
    t = np.arange(chunk)[:, None]
    s = np.arange(chunk)[None, :]
    masks = [(t == s)]
    m = chunk // 4
    while m >= 1:
        masks.append(((t ^ s) >> int(math.log2(m))) == 1)
        m //= 2
    return np.stack(masks).astype(np.float32)


def _split3(x):
    hi = x.astype(BF16)
    r1 = x - hi.astype(F32)
    mid = r1.astype(BF16)
    lo = (r1 - mid.astype(F32)).astype(BF16)
    return hi, mid, lo


def _hg_level_operands(q, kk, fd, b2, chunk):
    dk = q.shape[1]
    zeros = lambda n: jnp.zeros((n, dk), F32)
    qes, kes, halves = [q], [kk], [0]
    m = chunk // 2
    while m >= SUBLANES:
        qparts, kparts = [], []
        for s in range(0, chunk, 2 * m):
            ref = b2[s + m - 1:s + m, :]
            qparts += [zeros(m), q[s + m:s + 2 * m] * jnp.exp2(b2[s + m:s + 2 * m] - ref)]
            kparts += [kk[s:s + m] * jnp.exp2(ref - b2[s:s + m]), zeros(m)]
        qes.append(jnp.concatenate(qparts, axis=0))
        kes.append(jnp.concatenate(kparts, axis=0))
        halves.append(m)
        m //= 2
    row = lax.broadcasted_iota(jnp.int32, (chunk, dk), 0)
    b3 = b2.reshape(chunk // SUBLANES, SUBLANES, dk)
    ref = jnp.broadcast_to(b3[:, 3:4, :], b3.shape).reshape(chunk, dk)
    w = jnp.exp2(-jnp.abs(b2 - ref))
    upper = (row & 4) != 0
    qes.append(jnp.where(upper, q * w, 0.0))
    kes.append(jnp.where(upper, 0.0, kk * w))
    halves.append(4)
    f3 = fd.reshape(chunk // SUBLANES, SUBLANES, dk)
    f_prev = pltpu.roll(f3, 1, 1).reshape(chunk, dk)
    f_next = pltpu.roll(f3, SUBLANES - 1, 1).reshape(chunk, dk)
    r4 = row & 3
    qes.append(jnp.where(r4 >= 2, q * jnp.where(r4 == 3, fd * f_prev, fd), 0.0))
    kes.append(jnp.where(r4 >= 2, 0.0, kk * jnp.where(r4 == 0, f_next, 1.0)))
    halves.append(2)
    odd = (row & 1) != 0
    qes.append(jnp.where(odd, q * fd, 0.0))
    kes.append(jnp.where(odd, 0.0, kk))
    halves.append(1)
    return qes, kes, halves


def _hg_core(zq, zf, v, zg, lbc, gain, mc, pm_ref, st, chunk):
    rows, dk = zq.shape
    nch = rows // chunk
    log_lb, log_1m_lb, one_m_lb, lb = lbc[0:1], lbc[1:2], lbc[2:3], lbc[3:4]
    zf = zf.astype(F32)
    e = jnp.exp2(jnp.abs(zf) * (-LOG2E))
    den = 1.0 + e
    inv = 1.0 / den
    t = one_m_lb * jnp.where(zf >= 0.0, inv, e * inv)
    fd = lb + t
    kk = one_m_lb - t
    x2 = log_1m_lb + (jnp.minimum(zf, 0.0) - jnp.log(den))
    lf2 = (jnp.maximum(log_lb, x2)
           + jnp.log(1.0 + jnp.exp2(jnp.abs(log_lb - x2) * (-LOG2E)))) * LOG2E
    q = _silu(zq.astype(F32))

    w3 = jnp.concatenate(
        [jnp.concatenate([p[c * chunk:(c + 1) * chunk] for c in range(nch)], axis=1)
         for p in _split3(lf2)], axis=0)
    xw = _dot(mc, w3)

    ps, qds, upds, decs = [], [], [], []
    for c in range(nch):
        sl = slice(c * chunk, (c + 1) * chunk)
        b2 = xw[:, c * dk:(c + 1) * dk]
        qc, kc = q[sl], kk[sl]
        qes, kes, halves = _hg_level_operands(qc, kc, fd[sl], b2, chunk)
        p = _dot_dims(qes[1].astype(BF16), kes[1].astype(BF16), NT_DIMS)
        for lv in [0] + list(range(2, len(halves))):
            s = _dot_dims(qes[lv].astype(BF16), kes[lv].astype(BF16), NT_DIMS)
            p = p + s * pm_ref[0 if lv == 0 else lv - 1]
        ps.append(p.astype(BF16))
        qds.append((qc * jnp.exp2(b2)).astype(BF16))
        b_last = b2[chunk - 1:chunk, :]
        k_dec = (kc * jnp.exp2(b_last - b2)).astype(BF16)
        upds.append(_dot_dims(v[sl], k_dec, TN_DIMS))
        decs.append(jnp.exp2(b_last))

    outs = []
    for c in range(nch):
        sl = slice(c * chunk, (c + 1) * chunk)
        outs.append(_dot(ps[c], v[sl]) + _dot_dims(qds[c], st.astype(BF16), NT_DIMS))
        st = st * decs[c] + upds[c]

    o = jnp.concatenate(outs, axis=0)
    o = o * lax.rsqrt(jnp.mean(o * o, axis=-1, keepdims=True) + NORM_EPS) * gain
    return o * _silu(zg.astype(F32)), st


def _hgrn2_kernel(q_ref, f_ref, i_ref, g_ref, lbc_ref, gain_ref, mc_ref, pm_ref, o_ref, st_sc, *, chunk, sub):
    tt = q_ref.shape[0]
    dk = HG_HEAD_DIM
    n_heads = st_sc.shape[0]

    @pl.when(pl.program_id(2) == 0)
    def _():
        st_sc[...] = jnp.zeros(st_sc.shape, F32)

    def body(j, carry):
        rows = pl.ds(pl.multiple_of(j * sub, sub), sub)
        for hh in range(n_heads):
            cols = slice(hh * dk, (hh + 1) * dk)
            y, st = _hg_core(q_ref[rows, cols], f_ref[rows, cols], i_ref[rows, cols], g_ref[rows, cols],
                             lbc_ref[hh], gain_ref[hh], mc_ref[...], pm_ref, st_sc[hh], chunk)
            st_sc[hh] = st
            o_ref[rows, cols] = y.astype(o_ref.dtype)
        return carry

    lax.fori_loop(0, tt // sub, body, 0)


def _hgrn2(z, lbc, gain, *, batch, seq, d_hg, col0, tt, sub, chunk):
    dk = HG_HEAD_DIM
    heads = d_hg // dk
    hp = _tile(heads, HG_HEADS_PER_STEP)
    wb = hp * dk
    tt = min(tt, seq)
    sub = min(sub, tt)
    chunk = min(chunk, sub)
    assert chunk >= 2 * SUBLANES and tt % sub == 0 and sub % chunk == 0
    nt = seq // tt
    T = batch * seq
    mc = jnp.asarray(_hg_constants(chunk), BF16)
    pm = jnp.asarray(_hg_masks(chunk), F32)
    assert col0 % wb == 0 and d_hg % wb == 0

    def zspec(j):
        return pl.BlockSpec((tt, wb), lambda b, h, t: (b * nt + t, (col0 + j * d_hg) // wb + h))

    return pl.pallas_call(
        functools.partial(_hgrn2_kernel, chunk=chunk, sub=sub),
        grid=(batch, heads // hp, nt),
        in_specs=[zspec(0), zspec(1), zspec(2), zspec(3),
                  pl.BlockSpec((hp, 4, dk), lambda b, h, t: (h, 0, 0)),
                  pl.BlockSpec((hp, 1, dk), lambda b, h, t: (h, 0, 0)),
                  pl.BlockSpec(mc.shape, lambda b, h, t: (0, 0)),
                  pl.BlockSpec(pm.shape, lambda b, h, t: (0, 0, 0))],
        out_specs=pl.BlockSpec((tt, wb), lambda b, h, t: (b * nt + t, h)),
        out_shape=jax.ShapeDtypeStruct((T, d_hg), BF16),
        scratch_shapes=[pltpu.VMEM((hp, dk, dk), F32)],
        compiler_params=pltpu.CompilerParams(
            dimension_semantics=("parallel", "parallel", "arbitrary")),
        name="hgrn2",
    )(z, z, z, z, lbc, gain.reshape(heads, 1, dk), mc, pm)


def _lb_kernel(p_ref, o_ref):
    depth = p_ref.shape[0]
    p = p_ref[...]
    mx = jnp.max(p, axis=0, keepdims=True)
    ex = jnp.exp(p - mx)
    sm = ex / jnp.sum(ex, axis=0, keepdims=True)
    c0 = sm[0:1, :]
    c = c0
    for l in range(depth):
        if l > 0:
            c = c + sm[l:l + 1, :]
        lb = c - c0
        o_ref[l, 0:1, :] = jnp.log(lb)
        o_ref[l, 1:2, :] = jnp.log1p(-lb)
        o_ref[l, 2:3, :] = 1.0 - lb
        o_ref[l, 3:4, :] = lb


def _lb_constants(lb_param):
    depth, d_hg = lb_param.shape
    return pl.pallas_call(
        _lb_kernel,
        out_shape=jax.ShapeDtypeStruct((depth, 4, d_hg), F32),
        name="hgrn2_lower_bounds",
    )(lb_param)


def kernel(x, ffn1_norm, ffn1_w_gate, ffn1_w_up, ffn1_w_down, mix_norm, w_in, conv_w, conv_b, rg_wa, rg_ba, rg_wx, rg_bx, rg_lambda, hg_lower_bounds, hg_out_norm, w_out_a, w_out_b, w_out, ffn2_norm, ffn2_w_gate, ffn2_w_up, ffn2_w_down, final_norm):
    batch, seq, d_model = x.shape
    depth = w_in.shape[0]
    d_rg = rg_lambda.shape[1]
    d_hg = hg_lower_bounds.shape[1]
    heads_hg = d_hg // HG_HEAD_DIM
    T = batch * seq
    tm, tn = ROW_TILE, COL_TILE

    lbc = _lb_constants(hg_lower_bounds)
    lbc = lbc.reshape(depth, 4, heads_hg, HG_HEAD_DIM).transpose(0, 2, 1, 3)

    bf = lambda w: w.astype(BF16)
    names = ("wg1", "wu1", "wd1", "w_in", "wa", "wb", "w_out", "wg2", "wu2", "wd2")
    stacks = dict(zip(names, (ffn1_w_gate, ffn1_w_up, ffn1_w_down, w_in, w_out_a, w_out_b, w_out,
                              ffn2_w_gate, ffn2_w_up, ffn2_w_down)))
    wt = {(k, 0): bf(stacks[k][0]) for k in ("wg1", "wu1", "wd1")}
    rg_wa16, rg_wx16 = bf(rg_wa), bf(rg_wx)

    def run(fn, to_cast, *args, **kw):
        todo = [(k, ll) for k, ll in to_cast if ll < depth and (k, ll) not in wt]
        out, cast = fn(*args, [(stacks[k], ll) for k, ll in todo], **kw)
        wt.update(zip(todo, cast))
        return out

    h = x.reshape(T, d_model)
    for l in range(depth):
        early = [("w_in", l), ("wa", l), ("wb", l)]
        act = run(_norm_mm, [("wg1", l + 1), ("wu1", l + 1)] + early,
                  h, ffn1_norm[l], [wt["wg1", l], wt["wu1", l]], tm=tm // 2, tn=tn)
        early = [("w_out", l), ("wg2", l), ("wu2", l), ("wd2", l)]
        h = run(_mm_res, [("wd1", l + 1)] + early,
                act, wt["wd1", l], h, FFN_RES_WEIGHT, tm=tm, tn=tn, in_place=l > 0)

        z = run(_norm_mm, [("w_in", l + 1)], h, mix_norm[l], [wt["w_in", l]], tm=tm // 2, tn=2 * tn)
        y_a = _rglru(z, conv_w[l], conv_b[l], rg_wa16[l], rg_ba[l], rg_wx16[l], rg_bx[l], rg_lambda[l],
                     batch=batch, seq=seq, d_rg=d_rg, tt=SEQ_TILE)
        y_b = _hgrn2(z, lbc[l], hg_out_norm[l], batch=batch, seq=seq, d_hg=d_hg,
                     col0=2 * d_rg, tt=HG_SEQ_TILE, sub=HG_SUB, chunk=HG_CHUNK)
        ga_col = 2 * d_rg + 4 * d_hg
        merged = run(_merge, [("wa", l + 1), ("wb", l + 1)], y_a, y_b, wt["wa", l], wt["wb", l], z,
                     ga_col, ga_col + d_model, tm=tm, tn=tn)
        h = run(_mm_res, [("w_out", l + 1)], merged, wt["w_out", l], h, 1.0, tm=tm, tn=tn, in_place=True)

        act = run(_norm_mm, [("wg2", l + 1), ("wu2", l + 1)],
                  h, ffn2_norm[l], [wt["wg2", l], wt["wu2", l]], tm=tm // 2, tn=tn)
        h = run(_mm_res, [("wd2", l + 1)], act, wt["wd2", l], h, FFN_RES_WEIGHT, tm=tm, tn=tn, in_place=True)
    out = _rmsnorm(h, final_norm, tr=256)
    return out.reshape(batch, seq, d_model)
```

```python
import functools
import math

import numpy as np
import jax
import jax.numpy as jnp
from jax import lax
from jax.experimental import pallas as pl
from jax.experimental.pallas import tpu as pltpu

F32 = jnp.float32
BF16 = jnp.bfloat16

NORM_EPS = 1e-6
FFN_RES_WEIGHT = 0.5
RG_C = 8.0
HG_HEAD_DIM = 128

V7X_VMEM_BYTES = 64 * 1024 * 1024
SUBLANES = 8
LANES = 128
BF16_ROWS = 16
CAST_VMEM_BYTES = 12 * 1024 * 1024

ROW_TILE = 1024
COL_TILE = 512
NORM_ROWS = 128
SEQ_TILE = 512
HG_SEQ_TILE = 1024
HG_SUB = 512
HG_CHUNK = 64
HG_HEADS_PER_STEP = 4

LOG2E = 1.4426950408889634
NT_DIMS = (((1,), (1,)), ((), ()))
TN_DIMS = (((0,), (0,)), ((), ()))


def _vmem_limit(block_bytes):
    return int(min(V7X_VMEM_BYTES - 4 * 1024 * 1024, block_bytes * 1.3 + 8 * 1024 * 1024))


def _tile(n, pref):
    t = min(pref, n)
    while n % t:
        t //= 2
    return t


def _dot(a, b):
    return jnp.dot(a, b, preferred_element_type=F32)


def _dot_dims(a, b, dims):
    return lax.dot_general(a, b, dims, preferred_element_type=F32)


def _sigmoid(x):
    return 1.0 / (1.0 + jnp.exp2(x * (-LOG2E)))


def _silu(x):
    return x * _sigmoid(x)


def _norm_rows(h_ref, g_ref, u_ref, rows):
    D = h_ref.shape[1]
    ssq = None
    for c in range(0, D, D // 2):
        x = h_ref[rows, c:c + D // 2]
        part = jnp.sum(x * x, axis=-1, keepdims=True)
        ssq = part if ssq is None else ssq + part
    scale = lax.rsqrt(ssq * (1.0 / D) + NORM_EPS)
    for c in range(0, D, D // 4):
        cols = slice(c, c + D // 4)
        u_ref[rows, cols] = (h_ref[rows, cols] * scale * g_ref[:, cols]).astype(BF16)


def _cast_rows(rows, cols, steps, n_casts):
    cap = CAST_VMEM_BYTES // (n_casts * 12 * cols)
    pr = BF16_ROWS
    for cand in range(BF16_ROWS, rows + 1, BF16_ROWS):
        if rows % cand == 0 and (cand <= cap or rows // pr > steps):
            pr = cand
    return pr


def _cast_plan(casts, grid):
    steps = grid[0] * grid[1]
    in_specs, out_specs, out_shapes, operands, n_pieces = [], [], [], [], []
    for stack, layer in casts:
        _, R, C = stack.shape
        pr = _cast_rows(R, C, steps, len(casts))
        pieces = R // pr
        assert pieces <= steps

        def piece(m, n, pieces=pieces):
            return jnp.minimum(m * grid[1] + n, pieces - 1)

        in_specs.append(pl.BlockSpec((None, pr, C), lambda m, n, layer=layer, piece=piece: (layer, piece(m, n), 0)))
        out_specs.append(pl.BlockSpec((pr, C), lambda m, n, piece=piece: (piece(m, n), 0)))
        out_shapes.append(jax.ShapeDtypeStruct((R, C), BF16))
        operands.append(stack)
        n_pieces.append(pieces)
    return in_specs, out_specs, out_shapes, operands, tuple(n_pieces)


def _cast_bytes(casts, grid):
    steps = grid[0] * grid[1]
    total = 0
    for stack, _ in casts:
        _, R, C = stack.shape
        total += 2 * _cast_rows(R, C, steps, len(casts)) * C * (4 + 2)
    return total


def _do_casts(src_refs, dst_refs, n_pieces):
    step = pl.program_id(0) * pl.num_programs(1) + pl.program_id(1)
    for s, d, pieces in zip(src_refs, dst_refs, n_pieces):
        @pl.when(step < pieces)
        def _(s=s, d=d):
            d[...] = s[...].astype(BF16)


def _split_refs(refs, n_in, n_cast):
    ins = refs[:n_in]
    cast_src = refs[n_in:n_in + n_cast]
    out = refs[n_in + n_cast]
    cast_dst = refs[n_in + n_cast + 1:n_in + 2 * n_cast + 1]
    scratch = refs[n_in + 2 * n_cast + 1:]
    return ins, cast_src, out, cast_dst, scratch


def _norm_mm_kernel(*refs, n_w, cast_pieces):
    (h_ref, g_ref, *w_refs), cast_src, o_ref, cast_dst, (u_sc,) = _split_refs(refs, 2 + n_w, len(cast_pieces))

    @pl.when(pl.program_id(1) == 0)
    def _():
        rows = h_ref.shape[0]
        step = min(NORM_ROWS, rows)

        def body(i, c):
            _norm_rows(h_ref, g_ref, u_sc, pl.ds(pl.multiple_of(i * step, step), step))
            return c

        lax.fori_loop(0, rows // step, body, 0)

    u = u_sc[...]
    if n_w == 2:
        g = _dot(u, w_refs[0][...])
        up = _dot(u, w_refs[1][...])
        o_ref[...] = (_silu(g) * up).astype(o_ref.dtype)
    else:
        o_ref[...] = _dot(u, w_refs[0][...]).astype(o_ref.dtype)
    _do_casts(cast_src, cast_dst, cast_pieces)


def _norm_mm(h, gain, ws, casts, *, tm, tn):
    T, D = h.shape
    N = ws[0].shape[1]
    tm = _tile(T, tm)
    tn = _tile(N, tn)
    grid = (T // tm, N // tn)
    w_spec = pl.BlockSpec((D, tn), lambda m, n: (0, n))
    c_in, c_out, c_shapes, c_ops, c_pieces = _cast_plan(casts, grid)
    est = (2 * tm * D * 4 + tm * D * 2 + len(ws) * 2 * D * tn * 2 + 2 * tm * tn * 2
           + (1 + len(ws)) * tm * tn * 4 + _cast_bytes(casts, grid))
    outs = pl.pallas_call(
        functools.partial(_norm_mm_kernel, n_w=len(ws), cast_pieces=c_pieces),
        grid=grid,
        in_specs=[pl.BlockSpec((tm, D), lambda m, n: (m, 0)),
                  pl.BlockSpec((1, D), lambda m, n: (0, 0))] + [w_spec] * len(ws) + c_in,
        out_specs=[pl.BlockSpec((tm, tn), lambda m, n: (m, n))] + c_out,
        out_shape=[jax.ShapeDtypeStruct((T, N), BF16)] + c_shapes,
        scratch_shapes=[pltpu.VMEM((tm, D), BF16)],
        compiler_params=pltpu.CompilerParams(
            dimension_semantics=("arbitrary", "arbitrary"),
            vmem_limit_bytes=_vmem_limit(est)),
        name="norm_glu" if len(ws) == 2 else "norm_inproj",
    )(h, gain.reshape(1, D), *ws, *c_ops)
    return outs[0], outs[1:]


def _mm_res_kernel(*refs, scale, cast_pieces):
    (a_ref, w_ref, h_ref), cast_src, o_ref, cast_dst, _ = _split_refs(refs, 3, len(cast_pieces))
    acc = _dot(a_ref[...], w_ref[...])
    o_ref[...] = h_ref[...] + scale * acc
    _do_casts(cast_src, cast_dst, cast_pieces)


def _mm_res(a, w, h, scale, casts, *, tm, tn, in_place):
    T, K = a.shape
    N = w.shape[1]
    tm = _tile(T, tm)
    tn = _tile(N, tn)
    grid = (T // tm, N // tn)
    c_in, c_out, c_shapes, c_ops, c_pieces = _cast_plan(casts, grid)
    est = 2 * tm * K * 2 + 2 * K * tn * 2 + 4 * tm * tn * 4 + tm * tn * 4 + _cast_bytes(casts, grid)
    outs = pl.pallas_call(
        functools.partial(_mm_res_kernel, scale=scale, cast_pieces=c_pieces),
        grid=grid,
        in_specs=[pl.BlockSpec((tm, K), lambda m, n: (m, 0)),
                  pl.BlockSpec((K, tn), lambda m, n: (0, n)),
                  pl.BlockSpec((tm, tn), lambda m, n: (m, n))] + c_in,
        out_specs=[pl.BlockSpec((tm, tn), lambda m, n: (m, n))] + c_out,
        out_shape=[jax.ShapeDtypeStruct((T, N), F32)] + c_shapes,
        input_output_aliases={2: 0} if in_place else {},
        compiler_params=pltpu.CompilerParams(
            dimension_semantics=("arbitrary", "arbitrary"),
            vmem_limit_bytes=_vmem_limit(est)),
        name="mm_residual",
    )(a, w, h, *c_ops)
    return outs[0], outs[1:]


def _merge_kernel(*refs, cast_pieces):
    (ya_ref, yb_ref, wa_ref, wb_ref, ga_ref, gb_ref), cast_src, o_ref, cast_dst, _ = _split_refs(
        refs, 6, len(cast_pieces))
    a = _dot(ya_ref[...], wa_ref[...])
    b = _dot(yb_ref[...], wb_ref[...])
    ga = _sigmoid(ga_ref[...].astype(F32))
    gb = _sigmoid(gb_ref[...].astype(F32))
    o_ref[...] = (ga * a + gb * b).astype(o_ref.dtype)
    _do_casts(cast_src, cast_dst, cast_pieces)


def _merge(ya, yb, wa, wb, z, ga_col, gb_col, casts, *, tm, tn):
    T, KA = ya.shape
    KB = yb.shape[1]
    N = wa.shape[1]
    tm = _tile(T, tm)
    tn = _tile(N, tn)
    grid = (T // tm, N // tn)
    ga_blk, gb_blk = ga_col // tn, gb_col // tn
    assert ga_blk * tn == ga_col and gb_blk * tn == gb_col
    c_in, c_out, c_shapes, c_ops, c_pieces = _cast_plan(casts, grid)
    est = (2 * tm * (KA + KB) * 2 + 2 * (KA + KB) * tn * 2 + 6 * tm * tn * 2 + 4 * tm * tn * 4
           + _cast_bytes(casts, grid))
    outs = pl.pallas_call(
        functools.partial(_merge_kernel, cast_pieces=c_pieces),
        grid=grid,
        in_specs=[pl.BlockSpec((tm, KA), lambda m, n: (m, 0)),
                  pl.BlockSpec((tm, KB), lambda m, n: (m, 0)),
                  pl.BlockSpec((KA, tn), lambda m, n: (0, n)),
                  pl.BlockSpec((KB, tn), lambda m, n: (0, n)),
                  pl.BlockSpec((tm, tn), lambda m, n: (m, ga_blk + n)),
                  pl.BlockSpec((tm, tn), lambda m, n: (m, gb_blk + n))] + c_in,
        out_specs=[pl.BlockSpec((tm, tn), lambda m, n: (m, n))] + c_out,
        out_shape=[jax.ShapeDtypeStruct((T, N), BF16)] + c_shapes,
        compiler_params=pltpu.CompilerParams(
            dimension_semantics=("arbitrary", "arbitrary"),
            vmem_limit_bytes=_vmem_limit(est)),
        name="gated_merge",
    )(ya, yb, wa, wb, z, z, *c_ops)
    return outs[0], outs[1:]


def _rmsnorm_kernel(h_ref, g_ref, o_ref):
    x = h_ref[...]
    ms = jnp.mean(x * x, axis=-1, keepdims=True)
    o_ref[...] = x * lax.rsqrt(ms + NORM_EPS) * g_ref[...]


def _rmsnorm(h, gain, *, tr):
    T, D = h.shape
    tr = min(tr, T)
    return pl.pallas_call(
        _rmsnorm_kernel,
        grid=(T // tr,),
        in_specs=[pl.BlockSpec((tr, D), lambda i: (i, 0)),
                  pl.BlockSpec((1, D), lambda i: (0, 0))],
        out_specs=pl.BlockSpec((tr, D), lambda i: (i, 0)),
        out_shape=jax.ShapeDtypeStruct((T, D), F32),
        compiler_params=pltpu.CompilerParams(dimension_semantics=("parallel",)),
        name="final_norm",
    )(h, gain.reshape(1, D))


def _rglru_kernel(x_ref, y_ref, cw_ref, cb_ref, wa_ref, ba_ref, wx_ref, bx_ref, lam_ref,
                  o_ref, xp_sc, carry_sc, *, conv_width):
    tt, cb = x_ref.shape
    pad = SUBLANES

    @pl.when(pl.program_id(2) == 0)
    def _():
        xp_sc[...] = jnp.zeros((pad, cb), F32)
        carry_sc[...] = jnp.zeros((SUBLANES, cb), F32)

    x = x_ref[...].astype(F32)
    xpad = jnp.concatenate([xp_sc[...], x], axis=0)
    cw = cw_ref[0]
    xc = cb_ref[0] + x * cw[conv_width - 1:conv_width, :]
    for s in range(1, conv_width):
        xc = xc + pltpu.roll(xpad, s, 0)[pad:] * cw[conv_width - 1 - s:conv_width - s, :]
    xp_sc[...] = x[tt - pad:tt]

    xcb = xc.astype(BF16)
    r = _sigmoid(_dot(xcb, wa_ref[0]) + ba_ref[0])
    i = _sigmoid(_dot(xcb, wx_ref[0]) + bx_ref[0])
    log_a = (RG_C * jax.nn.log_sigmoid(lam_ref[0])) * r
    a = jnp.exp(log_a)
    b = jnp.sqrt(-jnp.tanh(log_a) * (a * a + 1.0)) * (i * xc)

    a = a.reshape(tt // SUBLANES, SUBLANES, cb)
    b = b.reshape(tt // SUBLANES, SUBLANES, cb)
    row8 = lax.broadcasted_iota(jnp.int32, a.shape, 1)
    for d in (1, 2, 4):
        keep = row8 >= d
        a_prev = jnp.where(keep, pltpu.roll(a, d, 1), 1.0)
        b_prev = jnp.where(keep, pltpu.roll(b, d, 1), 0.0)
        b = a * b_prev + b
        a = a * a_prev
    a = a.reshape(tt, cb)
    b = b.reshape(tt, cb)
    carry = carry_sc[...]
    hs = []
    for gi in range(tt // SUBLANES):
        g8 = slice(gi * SUBLANES, (gi + 1) * SUBLANES)
        hg = a[g8] * carry + b[g8]
        hs.append(hg)
        carry = jnp.broadcast_to(hg[SUBLANES - 1:SUBLANES, :], (SUBLANES, cb))
    carry_sc[...] = carry
    hseq = jnp.concatenate(hs, axis=0)
    o_ref[...] = (hseq * jax.nn.gelu(y_ref[...].astype(F32))).astype(o_ref.dtype)


def _rglru(z, cw, cb, wa, ba, wx, bx, lam, *, batch, seq, d_rg, tt):
    heads, blk = wa.shape[0], wa.shape[1]
    conv_width = cw.shape[0]
    tt = min(tt, seq)
    nt = seq // tt
    T = batch * seq
    vec = lambda v: v.reshape(heads, 1, blk)
    vspec = pl.BlockSpec((1, 1, blk), lambda b, h, t: (h, 0, 0))
    wspec = pl.BlockSpec((1, blk, blk), lambda b, h, t: (h, 0, 0))
    cwh = cw.reshape(conv_width, heads, blk).transpose(1, 0, 2)
    return pl.pallas_call(
        functools.partial(_rglru_kernel, conv_width=conv_width),
        grid=(batch, heads, nt),
        in_specs=[pl.BlockSpec((tt, blk), lambda b, h, t: (b * nt + t, h)),
                  pl.BlockSpec((tt, blk), lambda b, h, t: (b * nt + t, heads + h)),
                  pl.BlockSpec((1, conv_width, blk), lambda b, h, t: (h, 0, 0)),
                  vspec, wspec, vspec, wspec, vspec, vspec],
        out_specs=pl.BlockSpec((tt, blk), lambda b, h, t: (b * nt + t, h)),
        out_shape=jax.ShapeDtypeStruct((T, d_rg), BF16),
        scratch_shapes=[pltpu.VMEM((SUBLANES, blk), F32),
                        pltpu.VMEM((SUBLANES, blk), F32)],
        compiler_params=pltpu.CompilerParams(
            dimension_semantics=("parallel", "parallel", "arbitrary")),
        name="rglru",
    )(z, z, cwh, vec(cb), wa, vec(ba), wx, vec(bx), vec(lam))


def _hg_constants(chunk):
    t = np.arange(chunk)[:, None]
    r = np.arange(chunk)[None, :]
    tril = (r <= t).astype(np.float32)
    return np.concatenate([tril, tril, tril], axis=1)


def _hg_masks(chunk):
    t = np.arange(chunk)[:, None]
    s = np.arange(chunk)[None, :]
    masks = [(t == s)]
    m = chunk // 4
    while m >= 1:
        masks.append(((t ^ s) >> int(math.log2(m))) == 1)
        m //= 2
    return np.stack(masks).astype(np.float32)


def _split3(x):
    hi = x.astype(BF16)
    r1 = x - hi.astype(F32)
    mid = r1.astype(BF16)
    lo = (r1 - mid.astype(F32)).astype(BF16)
    return hi, mid, lo


def _hg_level_operands(q, kk, fd, b2, chunk):
    dk = q.shape[1]
    zeros = lambda n: jnp.zeros((n, dk), F32)
    qes, kes, halves = [q], [kk], [0]
    m = chunk // 2
    while m >= SUBLANES:
        qparts, kparts = [], []
        for s in range(0, chunk, 2 * m):
            ref = b2[s + m - 1:s + m, :]
            qparts += [zeros(m), q[s + m:s + 2 * m] * jnp.exp2(b2[s + m:s + 2 * m] - ref)]
            kparts += [kk[s:s + m] * jnp.exp2(ref - b2[s:s + m]), zeros(m)]
        qes.append(jnp.concatenate(qparts, axis=0))
        kes.append(jnp.concatenate(kparts, axis=0))
        halves.append(m)
        m //= 2
    row = lax.broadcasted_iota(jnp.int32, (chunk, dk), 0)
    b3 = b2.reshape(chunk // SUBLANES, SUBLANES, dk)
    ref = jnp.broadcast_to(b3[:, 3:4, :], b3.shape).reshape(chunk, dk)
    w = jnp.exp2(-jnp.abs(b2 - ref))
    upper = (row & 4) != 0
    qes.append(jnp.where(upper, q * w, 0.0))
    kes.append(jnp.where(upper, 0.0, kk * w))
    halves.append(4)
    f3 = fd.reshape(chunk // SUBLANES, SUBLANES, dk)
    f_prev = pltpu.roll(f3, 1, 1).reshape(chunk, dk)
    f_next = pltpu.roll(f3, SUBLANES - 1, 1).reshape(chunk, dk)
    r4 = row & 3
    qes.append(jnp.where(r4 >= 2, q * jnp.where(r4 == 3, fd * f_prev, fd), 0.0))
    kes.append(jnp.where(r4 >= 2, 0.0, kk * jnp.where(r4 == 0, f_next, 1.0)))
    halves.append(2)
    odd = (row & 1) != 0
    qes.append(jnp.where(odd, q * fd, 0.0))
    kes.append(jnp.where(odd, 0.0, kk))
    halves.append(1)
    return qes, kes, halves


def _hg_core(zq, zf, v, zg, lbc, gain, mc, pm_ref, st, chunk):
    rows, dk = zq.shape
    nch = rows // chunk
    log_lb, log_1m_lb, one_m_lb, lb = lbc[0:1], lbc[1:2], lbc[2:3], lbc[3:4]
    zf = zf.astype(F32)
    e = jnp.exp2(jnp.abs(zf) * (-LOG2E))
    den = 1.0 + e
    inv = 1.0 / den
    t = one_m_lb * jnp.where(zf >= 0.0, inv, e * inv)
    fd = lb + t
    kk = one_m_lb - t
    x2 = log_1m_lb + (jnp.minimum(zf, 0.0) - jnp.log(den))
    lf2 = (jnp.maximum(log_lb, x2)
           + jnp.log(1.0 + jnp.exp2(jnp.abs(log_lb - x2) * (-LOG2E)))) * LOG2E
    q = _silu(zq.astype(F32))

    w3 = jnp.concatenate(
        [jnp.concatenate([p[c * chunk:(c + 1) * chunk] for c in range(nch)], axis=1)
         for p in _split3(lf2)], axis=0)
    xw = _dot(mc, w3)

    ps, qds, upds, decs = [], [], [], []
    for c in range(nch):
        sl = slice(c * chunk, (c + 1) * chunk)
        b2 = xw[:, c * dk:(c + 1) * dk]
        qc, kc = q[sl], kk[sl]
        qes, kes, halves = _hg_level_operands(qc, kc, fd[sl], b2, chunk)
        p = _dot_dims(qes[1].astype(BF16), kes[1].astype(BF16), NT_DIMS)
        for lv in [0] + list(range(2, len(halves))):
            s = _dot_dims(qes[lv].astype(BF16), kes[lv].astype(BF16), NT_DIMS)
            p = p + s * pm_ref[0 if lv == 0 else lv - 1]
        ps.append(p.astype(BF16))
        qds.append((qc * jnp.exp2(b2)).astype(BF16))
        b_last = b2[chunk - 1:chunk, :]
        k_dec = (kc * jnp.exp2(b_last - b2)).astype(BF16)
        upds.append(_dot_dims(v[sl], k_dec, TN_DIMS))
        decs.append(jnp.exp2(b_last))

    outs = []
    for c in range(nch):
        sl = slice(c * chunk, (c + 1) * chunk)
        outs.append(_dot(ps[c], v[sl]) + _dot_dims(qds[c], st.astype(BF16), NT_DIMS))
        st = st * decs[c] + upds[c]

    o = jnp.concatenate(outs, axis=0)
    o = o * lax.rsqrt(jnp.mean(o * o, axis=-1, keepdims=True) + NORM_EPS) * gain
    return o * _silu(zg.astype(F32)), st


def _hgrn2_kernel(q_ref, f_ref, i_ref, g_ref, lbc_ref, gain_ref, mc_ref, pm_ref, o_ref, st_sc, *, chunk, sub):
    tt = q_ref.shape[0]
    dk = HG_HEAD_DIM
    n_heads = st_sc.shape[0]

    @pl.when(pl.program_id(2) == 0)
    def _():
        st_sc[...] = jnp.zeros(st_sc.shape, F32)

    def body(j, carry):
        rows = pl.ds(pl.multiple_of(j * sub, sub), sub)
        for hh in range(n_heads):
            cols = slice(hh * dk, (hh + 1) * dk)
            y, st = _hg_core(q_ref[rows, cols], f_ref[rows, cols], i_ref[rows, cols], g_ref[rows, cols],
                             lbc_ref[hh], gain_ref[hh], mc_ref[...], pm_ref, st_sc[hh], chunk)
            st_sc[hh] = st
            o_ref[rows, cols] = y.astype(o_ref.dtype)
        return carry

    lax.fori_loop(0, tt // sub, body, 0)


def _hgrn2(z, lbc, gain, *, batch, seq, d_hg, col0, tt, sub, chunk):
    dk = HG_HEAD_DIM
    heads = d_hg // dk
    hp = _tile(heads, HG_HEADS_PER_STEP)
    wb = hp * dk
    tt = min(tt, seq)
    sub = min(sub, tt)
    chunk = min(chunk, sub)
    assert chunk >= 2 * SUBLANES and tt % sub == 0 and sub % chunk == 0
    nt = seq // tt
    T = batch * seq
    mc = jnp.asarray(_hg_constants(chunk), BF16)
    pm = jnp.asarray(_hg_masks(chunk), F32)
    assert col0 % wb == 0 and d_hg % wb == 0

    def zspec(j):
        return pl.BlockSpec((tt, wb), lambda b, h, t: (b * nt + t, (col0 + j * d_hg) // wb + h))

    return pl.pallas_call(
        functools.partial(_hgrn2_kernel, chunk=chunk, sub=sub),
        grid=(batch, heads // hp, nt),
        in_specs=[zspec(0), zspec(1), zspec(2), zspec(3),
                  pl.BlockSpec((hp, 4, dk), lambda b, h, t: (h, 0, 0)),
                  pl.BlockSpec((hp, 1, dk), lambda b, h, t: (h, 0, 0)),
                  pl.BlockSpec(mc.shape, lambda b, h, t: (0, 0)),
                  pl.BlockSpec(pm.shape, lambda b, h, t: (0, 0, 0))],
        out_specs=pl.BlockSpec((tt, wb), lambda b, h, t: (b * nt + t, h)),
        out_shape=jax.ShapeDtypeStruct((T, d_hg), BF16),
        scratch_shapes=[pltpu.VMEM((hp, dk, dk), F32)],
        compiler_params=pltpu.CompilerParams(
            dimension_semantics=("parallel", "parallel", "arbitrary")),
        name="hgrn2",
    )(z, z, z, z, lbc, gain.reshape(heads, 1, dk), mc, pm)


def _lb_kernel(p_ref, o_ref):
    depth = p_ref.shape[0]
    p = p_ref[...]
    mx = jnp.max(p, axis=0, keepdims=True)
    ex = jnp.exp(p - mx)
    sm = ex / jnp.sum(ex, axis=0, keepdims=True)
    c0 = sm[0:1, :]
    c = c0
    for l in range(depth):
        if l > 0:
            c = c + sm[l:l + 1, :]
        lb = c - c0
        o_ref[l, 0:1, :] = jnp.log(lb)
        o_ref[l, 1:2, :] = jnp.log1p(-lb)
        o_ref[l, 2:3, :] = 1.0 - lb
        o_ref[l, 3:4, :] = lb


def _lb_constants(lb_param):
    depth, d_hg = lb_param.shape
    return pl.pallas_call(
        _lb_kernel,
        out_shape=jax.ShapeDtypeStruct((depth, 4, d_hg), F32),
        name="hgrn2_lower_bounds",
    )(lb_param)


def kernel(x, ffn1_norm, ffn1_w_gate, ffn1_w_up, ffn1_w_down, mix_norm, w_in, conv_w, conv_b, rg_wa, rg_ba, rg_wx, rg_bx, rg_lambda, hg_lower_bounds, hg_out_norm, w_out_a, w_out_b, w_out, ffn2_norm, ffn2_w_gate, ffn2_w_up, ffn2_w_down, final_norm):
    batch, seq, d_model = x.shape
    depth = w_in.shape[0]
    d_rg = rg_lambda.shape[1]
    d_hg = hg_lower_bounds.shape[1]
    heads_hg = d_hg // HG_HEAD_DIM
    T = batch * seq
    tm, tn = ROW_TILE, COL_TILE

    lbc = _lb_constants(hg_lower_bounds)
    lbc = lbc.reshape(depth, 4, heads_hg, HG_HEAD_DIM).transpose(0, 2, 1, 3)

    bf = lambda w: w.astype(BF16)
    names = ("wg1", "wu1", "wd1", "w_in", "wa", "wb", "w_out", "wg2", "wu2", "wd2")
    stacks = dict(zip(names, (ffn1_w_gate, ffn1_w_up, ffn1_w_down, w_in, w_out_a, w_out_b, w_out,
                              ffn2_w_gate, ffn2_w_up, ffn2_w_down)))
    wt = {(k, 0): bf(stacks[k][0]) for k in ("wg1", "wu1", "wd1")}
    rg_wa16, rg_wx16 = bf(rg_wa), bf(rg_wx)

    def run(fn, to_cast, *args, **kw):
        todo = [(k, ll) for k, ll in to_cast if ll < depth and (k, ll) not in wt]
        out, cast = fn(*args, [(stacks[k], ll) for k, ll in todo], **kw)
        wt.update(zip(todo, cast))
        return out

    h = x.reshape(T, d_model)
    for l in range(depth):
        early = [("w_in", l), ("wa", l), ("wb", l)]
        act = run(_norm_mm, [("wg1", l + 1), ("wu1", l + 1)] + early,
                  h, ffn1_norm[l], [wt["wg1", l], wt["wu1", l]], tm=tm // 2, tn=tn)
        early = [("w_out", l), ("wg2", l), ("wu2", l), ("wd2", l)]
        h = run(_mm_res, [("wd1", l + 1)] + early,
                act, wt["wd1", l], h, FFN_RES_WEIGHT, tm=tm, tn=tn, in_place=l > 0)

        z = run(_norm_mm, [("w_in", l + 1)], h, mix_norm[l], [wt["w_in", l]], tm=tm // 2, tn=2 * tn)
        y_a = _rglru(z, conv_w[l], conv_b[l], rg_wa16[l], rg_ba[l], rg_wx16[l], rg_bx[l], rg_lambda[l],
                     batch=batch, seq=seq, d_rg=d_rg, tt=SEQ_TILE)
        y_b = _hgrn2(z, lbc[l], hg_out_norm[l], batch=batch, seq=seq, d_hg=d_hg,
                     col0=2 * d_rg, tt=HG_SEQ_TILE, sub=HG_SUB, chunk=HG_CHUNK)
        ga_col = 2 * d_rg + 4 * d_hg
        merged = run(_merge, [("wa", l + 1), ("wb", l + 1)], y_a, y_b, wt["wa", l], wt["wb", l], z,
                     ga_col, ga_col + d_model, tm=tm, tn=tn)
        h = run(_mm_res, [("w_out", l + 1)], merged, wt["w_out", l], h, 1.0, tm=tm, tn=tn, in_place=True)

        act = run(_norm_mm, [("wg2", l + 1), ("wu2", l + 1)],
                  h, ffn2_norm[l], [wt["wg2", l], wt["wu2", l]], tm=tm // 2, tn=tn)
        h = run(_mm_res, [("wd2", l + 1)], act, wt["wd2", l], h, FFN_RES_WEIGHT, tm=tm, tn=tn, in_place=True)
    out = _rmsnorm(h, final_norm, tr=256)
    return out.reshape(batch, seq, d_model)
```

```python
import functools
import math

import numpy as np
import jax
import jax.numpy as jnp
from jax import lax
from jax.experimental import pallas as pl
from jax.experimental.pallas import tpu as pltpu

F32 = jnp.float32
BF16 = jnp.bfloat16

NORM_EPS = 1e-6
FFN_RES_WEIGHT = 0.5
RG_C = 8.0
HG_HEAD_DIM = 128

V7X_VMEM_BYTES = 64 * 1024 * 1024
SUBLANES = 8
LANES = 128
BF16_ROWS = 16

ROW_TILE = 1024
COL_TILE = 512
NORM_ROWS = 128
SEQ_TILE = 512
HG_SEQ_TILE = 1024
HG_SUB = 512
HG_CHUNK = 64
HG_HEADS_PER_STEP = 4

LOG2E = 1.4426950408889634
NT_DIMS = (((1,), (1,)), ((), ()))
TN_DIMS = (((0,), (0,)), ((), ()))


def _vmem_limit(block_bytes):
    return int(min(V7X_VMEM_BYTES - 4 * 1024 * 1024, block_bytes * 1.3 + 8 * 1024 * 1024))


def _tile(n, pref):
    t = min(pref, n)
    while n % t:
        t //= 2
    return t


def _dot(a, b):
    return jnp.dot(a, b, preferred_element_type=F32)


def _dot_dims(a, b, dims):
    return lax.dot_general(a, b, dims, preferred_element_type=F32)


def _sigmoid(x):
    return 1.0 / (1.0 + jnp.exp2(x * (-LOG2E)))


def _silu(x):
    return x * _sigmoid(x)


def _norm_rows(h_ref, g_ref, u_ref, rows):
    D = h_ref.shape[1]
    ssq = None
    for c in range(0, D, D // 2):
        x = h_ref[rows, c:c + D // 2]
        part = jnp.sum(x * x, axis=-1, keepdims=True)
        ssq = part if ssq is None else ssq + part
    scale = lax.rsqrt(ssq * (1.0 / D) + NORM_EPS)
    for c in range(0, D, D // 4):
        cols = slice(c, c + D // 4)
        u_ref[rows, cols] = (h_ref[rows, cols] * scale * g_ref[:, cols]).astype(u_ref.dtype)


def _cast_rows(rows, steps):
    pr = BF16_ROWS
    while rows % pr or rows // pr > steps:
        pr += BF16_ROWS
    return pr


def _cast_plan(casts, grid):
    steps = grid[0] * grid[1]
    in_specs, out_specs, out_shapes, operands = [], [], [], []
    for stack, layer in casts:
        _, R, C = stack.shape
        pr = _cast_rows(R, steps)
        pieces = R // pr

        def piece(m, n, pieces=pieces):
            return jnp.minimum(m * grid[1] + n, pieces - 1)

        in_specs.append(pl.BlockSpec((None, pr, C), lambda m, n, layer=layer, piece=piece: (layer, piece(m, n), 0)))
        out_specs.append(pl.BlockSpec((pr, C), lambda m, n, piece=piece: (piece(m, n), 0)))
        out_shapes.append(jax.ShapeDtypeStruct((R, C), BF16))
        operands.append(stack)
    return in_specs, out_specs, out_shapes, operands


def _cast_bytes(casts, grid):
    steps = grid[0] * grid[1]
    total = 0
    for stack, _ in casts:
        _, R, C = stack.shape
        total += 2 * _cast_rows(R, steps) * C * (4 + 2)
    return total


def _do_casts(src_refs, dst_refs):
    for s, d in zip(src_refs, dst_refs):
        d[...] = s[...].astype(BF16)


def _split_refs(refs, n_in, n_cast):
    ins = refs[:n_in]
    cast_src = refs[n_in:n_in + n_cast]
    out = refs[n_in + n_cast]
    cast_dst = refs[n_in + n_cast + 1:n_in + 2 * n_cast + 1]
    scratch = refs[n_in + 2 * n_cast + 1:]
    return ins, cast_src, out, cast_dst, scratch


def _norm_mm_kernel(*refs, n_w, n_cast):
    (h_ref, g_ref, *w_refs), cast_src, o_ref, cast_dst, (u_sc,) = _split_refs(refs, 2 + n_w, n_cast)

    @pl.when(pl.program_id(1) == 0)
    def _():
        rows = h_ref.shape[0]
        step = min(NORM_ROWS, rows)

        def body(i, c):
            _norm_rows(h_ref, g_ref, u_sc, pl.ds(pl.multiple_of(i * step, step), step))
            return c

        lax.fori_loop(0, rows // step, body, 0)

    u = u_sc[...]
    if n_w == 2:
        g = _dot(u, w_refs[0][...])
        up = _dot(u, w_refs[1][...])
        o_ref[...] = (_silu(g) * up).astype(o_ref.dtype)
    else:
        o_ref[...] = _dot(u, w_refs[0][...]).astype(o_ref.dtype)
    _do_casts(cast_src, cast_dst)


def _norm_mm(h, gain, ws, casts, *, tm, tn):
    T, D = h.shape
    N = ws[0].shape[1]
    tm = _tile(T, tm)
    tn = _tile(N, tn)
    grid = (T // tm, N // tn)
    w_spec = pl.BlockSpec((D, tn), lambda m, n: (0, n))
    c_in, c_out, c_shapes, c_ops = _cast_plan(casts, grid)
    est = (2 * tm * D * 4 + tm * D * 2 + len(ws) * 2 * D * tn * 2 + 2 * tm * tn * 2
           + (1 + len(ws)) * tm * tn * 4 + _cast_bytes(casts, grid))
    outs = pl.pallas_call(
        functools.partial(_norm_mm_kernel, n_w=len(ws), n_cast=len(casts)),
        grid=grid,
        in_specs=[pl.BlockSpec((tm, D), lambda m, n: (m, 0)),
                  pl.BlockSpec((1, D), lambda m, n: (0, 0))] + [w_spec] * len(ws) + c_in,
        out_specs=[pl.BlockSpec((tm, tn), lambda m, n: (m, n))] + c_out,
        out_shape=[jax.ShapeDtypeStruct((T, N), BF16)] + c_shapes,
        scratch_shapes=[pltpu.VMEM((tm, D), BF16)],
        compiler_params=pltpu.CompilerParams(
            dimension_semantics=("arbitrary", "arbitrary"),
            vmem_limit_bytes=_vmem_limit(est)),
        name="norm_glu" if len(ws) == 2 else "norm_inproj",
    )(h, gain.reshape(1, D), *ws, *c_ops)
    return outs[0], outs[1:]


def _mm_res_kernel(*refs, scale, n_cast):
    (a_ref, w_ref, h_ref), cast_src, o_ref, cast_dst, _ = _split_refs(refs, 3, n_cast)
    acc = _dot(a_ref[...], w_ref[...])
    o_ref[...] = h_ref[...] + scale * acc
    _do_casts(cast_src, cast_dst)


def _mm_res(a, w, h, scale, casts, *, tm, tn, in_place):
    T, K = a.shape
    N = w.shape[1]
    tm = _tile(T, tm)
    tn = _tile(N, tn)
    grid = (T // tm, N // tn)
    c_in, c_out, c_shapes, c_ops = _cast_plan(casts, grid)
    est = 2 * tm * K * 2 + 2 * K * tn * 2 + 4 * tm * tn * 4 + tm * tn * 4 + _cast_bytes(casts, grid)
    outs = pl.pallas_call(
        functools.partial(_mm_res_kernel, scale=scale, n_cast=len(casts)),
        grid=grid,
        in_specs=[pl.BlockSpec((tm, K), lambda m, n: (m, 0)),
                  pl.BlockSpec((K, tn), lambda m, n: (0, n)),
                  pl.BlockSpec((tm, tn), lambda m, n: (m, n))] + c_in,
        out_specs=[pl.BlockSpec((tm, tn), lambda m, n: (m, n))] + c_out,
        out_shape=[jax.ShapeDtypeStruct((T, N), F32)] + c_shapes,
        input_output_aliases={2: 0} if in_place else {},
        compiler_params=pltpu.CompilerParams(
            dimension_semantics=("arbitrary", "arbitrary"),
            vmem_limit_bytes=_vmem_limit(est)),
        name="mm_residual",
    )(a, w, h, *c_ops)
    return outs[0], outs[1:]


def _mm_res_norm_kernel(a_ref, w_ref, h_ref, g_ref, o_ref):
    n = pl.program_id(1)
    tn = h_ref.shape[1]
    acc = _dot(a_ref[...], w_ref[...])
    o_ref[:, pl.ds(pl.multiple_of(n * tn, tn), tn)] = h_ref[...] + FFN_RES_WEIGHT * acc

    @pl.when(n == pl.num_programs(1) - 1)
    def _():
        rows = o_ref.shape[0]
        step = min(NORM_ROWS, rows)

        def body(i, c):
            _norm_rows(o_ref, g_ref, o_ref, pl.ds(pl.multiple_of(i * step, step), step))
            return c

        lax.fori_loop(0, rows // step, body, 0)


def _mm_res_norm(a, w, h, gain, *, tm, tn):
    T, K = a.shape
    N = w.shape[1]
    tm = _tile(T, tm)
    tn = _tile(N, tn)
    est = 2 * tm * K * 2 + 2 * K * tn * 2 + 2 * tm * tn * 4 + 2 * tm * N * 4 + 2 * tm * tn * 4
    return pl.pallas_call(
        _mm_res_norm_kernel,
        grid=(T // tm, N // tn),
        in_specs=[pl.BlockSpec((tm, K), lambda m, n: (m, 0)),
                  pl.BlockSpec((K, tn), lambda m, n: (0, n)),
                  pl.BlockSpec((tm, tn), lambda m, n: (m, n)),
                  pl.BlockSpec((1, N), lambda m, n: (0, 0))],
        out_specs=pl.BlockSpec((tm, N), lambda m, n: (m, 0)),
        out_shape=jax.ShapeDtypeStruct((T, N), F32),
        compiler_params=pltpu.CompilerParams(
            dimension_semantics=("arbitrary", "arbitrary"),
            vmem_limit_bytes=_vmem_limit(est)),
        name="mm_residual_norm",
    )(a, w, h, gain.reshape(1, N))


def _merge_kernel(*refs, n_cast):
    (ya_ref, yb_ref, wa_ref, wb_ref, ga_ref, gb_ref), cast_src, o_ref, cast_dst, _ = _split_refs(refs, 6, n_cast)
    a = _dot(ya_ref[...], wa_ref[...])
    b = _dot(yb_ref[...], wb_ref[...])
    ga = _sigmoid(ga_ref[...].astype(F32))
    gb = _sigmoid(gb_ref[...].astype(F32))
    o_ref[...] = (ga * a + gb * b).astype(o_ref.dtype)
    _do_casts(cast_src, cast_dst)


def _merge(ya, yb, wa, wb, z, ga_col, gb_col, casts, *, tm, tn):
    T, KA = ya.shape
    KB = yb.shape[1]
    N = wa.shape[1]
    tm = _tile(T, tm)
    tn = _tile(N, tn)
    grid = (T // tm, N // tn)
    ga_blk, gb_blk = ga_col // tn, gb_col // tn
    assert ga_blk * tn == ga_col and gb_blk * tn == gb_col
    c_in, c_out, c_shapes, c_ops = _cast_plan(casts, grid)
    est = (2 * tm * (KA + KB) * 2 + 2 * (KA + KB) * tn * 2 + 6 * tm * tn * 2 + 4 * tm * tn * 4
           + _cast_bytes(casts, grid))
    outs = pl.pallas_call(
        functools.partial(_merge_kernel, n_cast=len(casts)),
        grid=grid,
        in_specs=[pl.BlockSpec((tm, KA), lambda m, n: (m, 0)),
                  pl.BlockSpec((tm, KB), lambda m, n: (m, 0)),
                  pl.BlockSpec((KA, tn), lambda m, n: (0, n)),
                  pl.BlockSpec((KB, tn), lambda m, n: (0, n)),
                  pl.BlockSpec((tm, tn), lambda m, n: (m, ga_blk + n)),
                  pl.BlockSpec((tm, tn), lambda m, n: (m, gb_blk + n))] + c_in,
        out_specs=[pl.BlockSpec((tm, tn), lambda m, n: (m, n))] + c_out,
        out_shape=[jax.ShapeDtypeStruct((T, N), BF16)] + c_shapes,
        compiler_params=pltpu.CompilerParams(
            dimension_semantics=("arbitrary", "arbitrary"),
            vmem_limit_bytes=_vmem_limit(est)),
        name="gated_merge",
    )(ya, yb, wa, wb, z, z, *c_ops)
    return outs[0], outs[1:]


def _rglru_kernel(x_ref, y_ref, cw_ref, cb_ref, wa_ref, ba_ref, wx_ref, bx_ref, lam_ref,
                  o_ref, xp_sc, carry_sc, *, conv_width):
    tt, cb = x_ref.shape
    pad = SUBLANES

    @pl.when(pl.program_id(2) == 0)
    def _():
        xp_sc[...] = jnp.zeros((pad, cb), F32)
        carry_sc[...] = jnp.zeros((SUBLANES, cb), F32)

    x = x_ref[...].astype(F32)
    xpad = jnp.concatenate([xp_sc[...], x], axis=0)
    cw = cw_ref[0]
    xc = cb_ref[0] + x * cw[conv_width - 1:conv_width, :]
    for s in range(1, conv_width):
        xc = xc + pltpu.roll(xpad, s, 0)[pad:] * cw[conv_width - 1 - s:conv_width - s, :]
    xp_sc[...] = x[tt - pad:tt]

    xcb = xc.astype(BF16)
    r = _sigmoid(_dot(xcb, wa_ref[0]) + ba_ref[0])
    i = _sigmoid(_dot(xcb, wx_ref[0]) + bx_ref[0])
    log_a = (RG_C * jax.nn.log_sigmoid(lam_ref[0])) * r
    a = jnp.exp(log_a)
    b = jnp.sqrt(-jnp.tanh(log_a) * (a * a + 1.0)) * (i * xc)

    a = a.reshape(tt // SUBLANES, SUBLANES, cb)
    b = b.reshape(tt // SUBLANES, SUBLANES, cb)
    row8 = lax.broadcasted_iota(jnp.int32, a.shape, 1)
    for d in (1, 2, 4):
        keep = row8 >= d
        a_prev = jnp.where(keep, pltpu.roll(a, d, 1), 1.0)
        b_prev = jnp.where(keep, pltpu.roll(b, d, 1), 0.0)
        b = a * b_prev + b
        a = a * a_prev
    a = a.reshape(tt, cb)
    b = b.reshape(tt, cb)
    carry = carry_sc[...]
    hs = []
    for gi in range(tt // SUBLANES):
        g8 = slice(gi * SUBLANES, (gi + 1) * SUBLANES)
        hg = a[g8] * carry + b[g8]
        hs.append(hg)
        carry = jnp.broadcast_to(hg[SUBLANES - 1:SUBLANES, :], (SUBLANES, cb))
    carry_sc[...] = carry
    hseq = jnp.concatenate(hs, axis=0)
    o_ref[...] = (hseq * jax.nn.gelu(y_ref[...].astype(F32))).astype(o_ref.dtype)


def _rglru(z, cw, cb, wa, ba, wx, bx, lam, *, batch, seq, d_rg, tt):
    heads, blk = wa.shape[0], wa.shape[1]
    conv_width = cw.shape[0]
    tt = min(tt, seq)
    nt = seq // tt
    T = batch * seq
    vec = lambda v: v.reshape(heads, 1, blk)
    vspec = pl.BlockSpec((1, 1, blk), lambda b, h, t: (h, 0, 0))
    wspec = pl.BlockSpec((1, blk, blk), lambda b, h, t: (h, 0, 0))
    cwh = cw.reshape(conv_width, heads, blk).transpose(1, 0, 2)
    return pl.pallas_call(
        functools.partial(_rglru_kernel, conv_width=conv_width),
        grid=(batch, heads, nt),
        in_specs=[pl.BlockSpec((tt, blk), lambda b, h, t: (b * nt + t, h)),
                  pl.BlockSpec((tt, blk), lambda b, h, t: (b * nt + t, heads + h)),
                  pl.BlockSpec((1, conv_width, blk), lambda b, h, t: (h, 0, 0)),
                  vspec, wspec, vspec, wspec, vspec, vspec],
        out_specs=pl.BlockSpec((tt, blk), lambda b, h, t: (b * nt + t, h)),
        out_shape=jax.ShapeDtypeStruct((T, d_rg), BF16),
        scratch_shapes=[pltpu.VMEM((SUBLANES, blk), F32),
                        pltpu.VMEM((SUBLANES, blk), F32)],
        compiler_params=pltpu.CompilerParams(
            dimension_semantics=("parallel", "parallel", "arbitrary")),
        name="rglru",
    )(z, z, cwh, vec(cb), wa, vec(ba), wx, vec(bx), vec(lam))


def _hg_constants(chunk):
    t = np.arange(chunk)[:, None]
    r = np.arange(chunk)[None, :]
    tril = (r <= t).astype(np.float32)
    return np.concatenate([tril, tril, tril], axis=1)


def _hg_masks(chunk):
    t = np.arange(chunk)[:, None]
    s = np.arange(chunk)[None, :]
    masks = [(t == s)]
    m = chunk // 4
    while m >= 1:
        masks.append(((t ^ s) >> int(math.log2(m))) == 1)
        m //= 2
    return np.stack(masks).astype(np.float32)


def _split3(x):
    hi = x.astype(BF16)
    r1 = x - hi.astype(F32)
    mid = r1.astype(BF16)
    lo = (r1 - mid.astype(F32)).astype(BF16)
    return hi, mid, lo


def _hg_level_operands(q, kk, fd, b2, chunk):
    dk = q.shape[1]
    zeros = lambda n: jnp.zeros((n, dk), F32)
    qes, kes, halves = [q], [kk], [0]
    m = chunk // 2
    while m >= SUBLANES:
        qparts, kparts = [], []
        for s in range(0, chunk, 2 * m):
            ref = b2[s + m - 1:s + m, :]
            qparts += [zeros(m), q[s + m:s + 2 * m] * jnp.exp2(b2[s + m:s + 2 * m] - ref)]
            kparts += [kk[s:s + m] * jnp.exp2(ref - b2[s:s + m]), zeros(m)]
        qes.append(jnp.concatenate(qparts, axis=0))
        kes.append(jnp.concatenate(kparts, axis=0))
        halves.append(m)
        m //= 2
    row = lax.broadcasted_iota(jnp.int32, (chunk, dk), 0)
    b3 = b2.reshape(chunk // SUBLANES, SUBLANES, dk)
    ref = jnp.broadcast_to(b3[:, 3:4, :], b3.shape).reshape(chunk, dk)
    w = jnp.exp2(-jnp.abs(b2 - ref))
    upper = (row & 4) != 0
    qes.append(jnp.where(upper, q * w, 0.0))
    kes.append(jnp.where(upper, 0.0, kk * w))
    halves.append(4)
    f3 = fd.reshape(chunk // SUBLANES, SUBLANES, dk)
    f_prev = pltpu.roll(f3, 1, 1).reshape(chunk, dk)
    f_next = pltpu.roll(f3, SUBLANES - 1, 1).reshape(chunk, dk)
    r4 = row & 3
    qes.append(jnp.where(r4 >= 2, q * jnp.where(r4 == 3, fd * f_prev, fd), 0.0))
    kes.append(jnp.where(r4 >= 2, 0.0, kk * jnp.where(r4 == 0, f_next, 1.0)))
    halves.append(2)
    odd = (row & 1) != 0
    qes.append(jnp.where(odd, q * fd, 0.0))
    kes.append(jnp.where(odd, 0.0, kk))
    halves.append(1)
    return qes, kes, halves


def _hg_core(zq, zf, v, zg, lbc, gain, mc, pm_ref, st, chunk):
    rows, dk = zq.shape
    nch = rows // chunk
    log_lb, log_1m_lb, one_m_lb, lb = lbc[0:1], lbc[1:2], lbc[2:3], lbc[3:4]
    zf = zf.astype(F32)
    e = jnp.exp2(jnp.abs(zf) * (-LOG2E))
    den = 1.0 + e
    inv = 1.0 / den
    t = one_m_lb * jnp.where(zf >= 0.0, inv, e * inv)
    fd = lb + t
    kk = one_m_lb - t
    x2 = log_1m_lb + (jnp.minimum(zf, 0.0) - jnp.log(den))
    lf2 = (jnp.maximum(log_lb, x2)
           + jnp.log(1.0 + jnp.exp2(jnp.abs(log_lb - x2) * (-LOG2E)))) * LOG2E
    q = _silu(zq.astype(F32))

    w3 = jnp.concatenate(
        [jnp.concatenate([p[c * chunk:(c + 1) * chunk] for c in range(nch)], axis=1)
         for p in _split3(lf2)], axis=0)
    xw = _dot(mc, w3)

    ps, qds, upds, decs = [], [], [], []
    for c in range(nch):
        sl = slice(c * chunk, (c + 1) * chunk)
        b2 = xw[:, c * dk:(c + 1) * dk]
        qc, kc = q[sl], kk[sl]
        qes, kes, halves = _hg_level_operands(qc, kc, fd[sl], b2, chunk)
        p = _dot_dims(qes[1].astype(BF16), kes[1].astype(BF16), NT_DIMS)
        for lv in [0] + list(range(2, len(halves))):
            s = _dot_dims(qes[lv].astype(BF16), kes[lv].astype(BF16), NT_DIMS)
            p = p + s * pm_ref[0 if lv == 0 else lv - 1]
        ps.append(p.astype(BF16))
        qds.append((qc * jnp.exp2(b2)).astype(BF16))
        b_last = b2[chunk - 1:chunk, :]
        k_dec = (kc * jnp.exp2(b_last - b2)).astype(BF16)
        upds.append(_dot_dims(v[sl], k_dec, TN_DIMS))
        decs.append(jnp.exp2(b_last))

    outs = []
    for c in range(nch):
        sl = slice(c * chunk, (c + 1) * chunk)
        outs.append(_dot(ps[c], v[sl]) + _dot_dims(qds[c], st.astype(BF16), NT_DIMS))
        st = st * decs[c] + upds[c]

    o = jnp.concatenate(outs, axis=0)
    o = o * lax.rsqrt(jnp.mean(o * o, axis=-1, keepdims=True) + NORM_EPS) * gain
    return o * _silu(zg.astype(F32)), st


def _hgrn2_kernel(q_ref, f_ref, i_ref, g_ref, lbc_ref, gain_ref, mc_ref, pm_ref, o_ref, st_sc, *, chunk, sub):
    tt = q_ref.shape[0]
    dk = HG_HEAD_DIM
    n_heads = st_sc.shape[0]

    @pl.when(pl.program_id(2) == 0)
    def _():
        st_sc[...] = jnp.zeros(st_sc.shape, F32)

    def body(j, carry):
        rows = pl.ds(pl.multiple_of(j * sub, sub), sub)
        for hh in range(n_heads):
            cols = slice(hh * dk, (hh + 1) * dk)
            y, st = _hg_core(q_ref[rows, cols], f_ref[rows, cols], i_ref[rows, cols], g_ref[rows, cols],
                             lbc_ref[hh], gain_ref[hh], mc_ref[...], pm_ref, st_sc[hh], chunk)
            st_sc[hh] = st
            o_ref[rows, cols] = y.astype(o_ref.dtype)
        return carry

    lax.fori_loop(0, tt // sub, body, 0)


def _hgrn2(z, lbc, gain, *, batch, seq, d_hg, col0, tt, sub, chunk):
    dk = HG_HEAD_DIM
    heads = d_hg // dk
    hp = _tile(heads, HG_HEADS_PER_STEP)
    wb = hp * dk
    tt = min(tt, seq)
    sub = min(sub, tt)
    chunk = min(chunk, sub)
    assert chunk >= 2 * SUBLANES and tt % sub == 0 and sub % chunk == 0
    nt = seq // tt
    T = batch * seq
    mc = jnp.asarray(_hg_constants(chunk), BF16)
    pm = jnp.asarray(_hg_masks(chunk), F32)
    assert col0 % wb == 0 and d_hg % wb == 0

    def zspec(j):
        return pl.BlockSpec((tt, wb), lambda b, h, t: (b * nt + t, (col0 + j * d_hg) // wb + h))

    return pl.pallas_call(
        functools.partial(_hgrn2_kernel, chunk=chunk, sub=sub),
        grid=(batch, heads // hp, nt),
        in_specs=[zspec(0), zspec(1), zspec(2), zspec(3),
                  pl.BlockSpec((hp, 4, dk), lambda b, h, t: (h, 0, 0)),
                  pl.BlockSpec((hp, 1, dk), lambda b, h, t: (h, 0, 0)),
                  pl.BlockSpec(mc.shape, lambda b, h, t: (0, 0)),
                  pl.BlockSpec(pm.shape, lambda b, h, t: (0, 0, 0))],
        out_specs=pl.BlockSpec((tt, wb), lambda b, h, t: (b * nt + t, h)),
        out_shape=jax.ShapeDtypeStruct((T, d_hg), BF16),
        scratch_shapes=[pltpu.VMEM((hp, dk, dk), F32)],
        compiler_params=pltpu.CompilerParams(
            dimension_semantics=("parallel", "parallel", "arbitrary")),
        name="hgrn2",
    )(z, z, z, z, lbc, gain.reshape(heads, 1, dk), mc, pm)


def _lb_kernel(p_ref, o_ref):
    depth = p_ref.shape[0]
    p = p_ref[...]
    mx = jnp.max(p, axis=0, keepdims=True)
    ex = jnp.exp(p - mx)
    sm = ex / jnp.sum(ex, axis=0, keepdims=True)
    c0 = sm[0:1, :]
    c = c0
    for l in range(depth):
        if l > 0:
            c = c + sm[l:l + 1, :]
        lb = c - c0
        o_ref[l, 0:1, :] = jnp.log(lb)
        o_ref[l, 1:2, :] = jnp.log1p(-lb)
        o_ref[l, 2:3, :] = 1.0 - lb
        o_ref[l, 3:4, :] = lb


def _lb_constants(lb_param):
    depth, d_hg = lb_param.shape
    return pl.pallas_call(
        _lb_kernel,
        out_shape=jax.ShapeDtypeStruct((depth, 4, d_hg), F32),
        name="hgrn2_lower_bounds",
    )(lb_param)


def kernel(x, ffn1_norm, ffn1_w_gate, ffn1_w_up, ffn1_w_down, mix_norm, w_in, conv_w, conv_b, rg_wa, rg_ba, rg_wx, rg_bx, rg_lambda, hg_lower_bounds, hg_out_norm, w_out_a, w_out_b, w_out, ffn2_norm, ffn2_w_gate, ffn2_w_up, ffn2_w_down, final_norm):
    batch, seq, d_model = x.shape
    depth = w_in.shape[0]
    d_rg = rg_lambda.shape[1]
    d_hg = hg_lower_bounds.shape[1]
    heads_hg = d_hg // HG_HEAD_DIM
    T = batch * seq
    tm, tn = ROW_TILE, COL_TILE

    lbc = _lb_constants(hg_lower_bounds)
    lbc = lbc.reshape(depth, 4, heads_hg, HG_HEAD_DIM).transpose(0, 2, 1, 3)

    bf = lambda w: w.astype(BF16)
    names = ("wg1", "wu1", "wd1", "w_in", "wa", "wb", "w_out", "wg2", "wu2", "wd2")
    stacks = dict(zip(names, (ffn1_w_gate, ffn1_w_up, ffn1_w_down, w_in, w_out_a, w_out_b, w_out,
                              ffn2_w_gate, ffn2_w_up, ffn2_w_down)))
    wt = {(k, 0): bf(stacks[k][0]) for k in ("wg1", "wu1", "wd1")}
    rg_wa16, rg_wx16 = bf(rg_wa), bf(rg_wx)

    def run(fn, to_cast, *args, **kw):
        todo = [(k, ll) for k, ll in to_cast if ll < depth and (k, ll) not in wt]
        out, cast = fn(*args, [(stacks[k], ll) for k, ll in todo], **kw)
        wt.update(zip(todo, cast))
        return out

    h = x.reshape(T, d_model)
    for l in range(depth):
        early = [("w_in", l), ("wa", l), ("wb", l)]
        act = run(_norm_mm, [("wg1", l + 1), ("wu1", l + 1)] + early,
                  h, ffn1_norm[l], [wt["wg1", l], wt["wu1", l]], tm=tm // 2, tn=tn)
        early = [("w_out", l), ("wg2", l), ("wu2", l), ("wd2", l)]
        h = run(_mm_res, [("wd1", l + 1)] + early,
                act, wt["wd1", l], h, FFN_RES_WEIGHT, tm=tm, tn=tn, in_place=l > 0)

        z = run(_norm_mm, [("w_in", l + 1)], h, mix_norm[l], [wt["w_in", l]], tm=tm // 2, tn=2 * tn)
        y_a = _rglru(z, conv_w[l], conv_b[l], rg_wa16[l], rg_ba[l], rg_wx16[l], rg_bx[l], rg_lambda[l],
                     batch=batch, seq=seq, d_rg=d_rg, tt=SEQ_TILE)
        y_b = _hgrn2(z, lbc[l], hg_out_norm[l], batch=batch, seq=seq, d_hg=d_hg,
                     col0=2 * d_rg, tt=HG_SEQ_TILE, sub=HG_SUB, chunk=HG_CHUNK)
        ga_col = 2 * d_rg + 4 * d_hg
        merged = run(_merge, [("wa", l + 1), ("wb", l + 1)], y_a, y_b, wt["wa", l], wt["wb", l], z,
                     ga_col, ga_col + d_model, tm=tm, tn=tn)
        h = run(_mm_res, [("w_out", l + 1)], merged, wt["w_out", l], h, 1.0, tm=tm, tn=tn, in_place=True)

        act = run(_norm_mm, [("wg2", l + 1), ("wu2", l + 1)],
                  h, ffn2_norm[l], [wt["wg2", l], wt["wu2", l]], tm=tm // 2, tn=tn)
        if l + 1 < depth:
            h = run(_mm_res, [("wd2", l + 1)], act, wt["wd2", l], h, FFN_RES_WEIGHT, tm=tm, tn=tn, in_place=True)
        else:
            out = _mm_res_norm(act, wt["wd2", l], h, final_norm, tm=tm // 2, tn=tn)
    return out.reshape(batch, seq, d_model)
```

```python
import functools
import math

import numpy as np
import jax
import jax.numpy as jnp
from jax import lax
from jax.experimental import pallas as pl
from jax.experimental.pallas import tpu as pltpu

F32 = jnp.float32
BF16 = jnp.bfloat16

NORM_EPS = 1e-6
FFN_RES_WEIGHT = 0.5
RG_C = 8.0
HG_HEAD_DIM = 128

V7X_VMEM_BYTES = 64 * 1024 * 1024
SUBLANES = 8
LANES = 128
BF16_ROWS = 16

ROW_TILE = 1024
COL_TILE = 512
NORM_ROWS = 128
SEQ_TILE = 512
HG_SEQ_TILE = 1024
HG_SUB = 512
HG_CHUNK = 64
HG_HEADS_PER_STEP = 4
RG_HEADS_PER_STEP = 2

LOG2E = 1.4426950408889634
NT_DIMS = (((1,), (1,)), ((), ()))
TN_DIMS = (((0,), (0,)), ((), ()))


def _vmem_limit(block_bytes):
    return int(min(V7X_VMEM_BYTES - 4 * 1024 * 1024, block_bytes * 1.3 + 8 * 1024 * 1024))


def _tile(n, pref):
    t = min(pref, n)
    while n % t:
        t //= 2
    return t


def _dot(a, b):
    return jnp.dot(a, b, preferred_element_type=F32)


def _dot_dims(a, b, dims):
    return lax.dot_general(a, b, dims, preferred_element_type=F32)


def _sigmoid(x):
    return 1.0 / (1.0 + jnp.exp2(x * (-LOG2E)))


def _silu(x):
    return x * _sigmoid(x)


def _norm_rows(h_ref, g_ref, u_ref, rows):
    D = h_ref.shape[1]
    ssq = None
    for c in range(0, D, D // 2):
        x = h_ref[rows, c:c + D // 2]
        part = jnp.sum(x * x, axis=-1, keepdims=True)
        ssq = part if ssq is None else ssq + part
    scale = lax.rsqrt(ssq * (1.0 / D) + NORM_EPS)
    for c in range(0, D, D // 4):
        cols = slice(c, c + D // 4)
        u_ref[rows, cols] = (h_ref[rows, cols] * scale * g_ref[:, cols]).astype(BF16)


def _cast_rows(rows, steps):
    pr = BF16_ROWS
    while rows % pr or rows // pr > steps:
        pr += BF16_ROWS
    return pr


def _cast_plan(casts, grid):
    steps = grid[0] * grid[1]
    in_specs, out_specs, out_shapes, operands = [], [], [], []
    for stack, layer in casts:
        _, R, C = stack.shape
        pr = _cast_rows(R, steps)
        pieces = R // pr

        def piece(m, n, pieces=pieces):
            return jnp.minimum(m * grid[1] + n, pieces - 1)

        in_specs.append(pl.BlockSpec((None, pr, C), lambda m, n, layer=layer, piece=piece: (layer, piece(m, n), 0)))
        out_specs.append(pl.BlockSpec((pr, C), lambda m, n, piece=piece: (piece(m, n), 0)))
        out_shapes.append(jax.ShapeDtypeStruct((R, C), BF16))
        operands.append(stack)
    return in_specs, out_specs, out_shapes, operands


def _cast_bytes(casts, grid):
    steps = grid[0] * grid[1]
    total = 0
    for stack, _ in casts:
        _, R, C = stack.shape
        total += 2 * _cast_rows(R, steps) * C * (4 + 2)
    return total


def _do_casts(src_refs, dst_refs):
    for s, d in zip(src_refs, dst_refs):
        d[...] = s[...].astype(BF16)


def _split_refs(refs, n_in, n_cast):
    ins = refs[:n_in]
    cast_src = refs[n_in:n_in + n_cast]
    out = refs[n_in + n_cast]
    cast_dst = refs[n_in + n_cast + 1:n_in + 2 * n_cast + 1]
    scratch = refs[n_in + 2 * n_cast + 1:]
    return ins, cast_src, out, cast_dst, scratch


def _norm_mm_kernel(*refs, n_w, n_cast):
    (h_ref, g_ref, *w_refs), cast_src, o_ref, cast_dst, (u_sc,) = _split_refs(refs, 2 + n_w, n_cast)

    @pl.when(pl.program_id(1) == 0)
    def _():
        rows = h_ref.shape[0]
        step = min(NORM_ROWS, rows)

        def body(i, c):
            _norm_rows(h_ref, g_ref, u_sc, pl.ds(pl.multiple_of(i * step, step), step))
            return c

        lax.fori_loop(0, rows // step, body, 0)

    u = u_sc[...]
    if n_w == 2:
        g = _dot(u, w_refs[0][...])
        up = _dot(u, w_refs[1][...])
        o_ref[...] = (_silu(g) * up).astype(o_ref.dtype)
    else:
        o_ref[...] = _dot(u, w_refs[0][...]).astype(o_ref.dtype)
    _do_casts(cast_src, cast_dst)


def _norm_mm(h, gain, ws, casts, *, tm, tn):
    T, D = h.shape
    N = ws[0].shape[1]
    tm = _tile(T, tm)
    tn = _tile(N, tn)
    grid = (T // tm, N // tn)
    w_spec = pl.BlockSpec((D, tn), lambda m, n: (0, n))
    c_in, c_out, c_shapes, c_ops = _cast_plan(casts, grid)
    est = (2 * tm * D * 4 + tm * D * 2 + len(ws) * 2 * D * tn * 2 + 2 * tm * tn * 2
           + (1 + len(ws)) * tm * tn * 4 + _cast_bytes(casts, grid))
    outs = pl.pallas_call(
        functools.partial(_norm_mm_kernel, n_w=len(ws), n_cast=len(casts)),
        grid=grid,
        in_specs=[pl.BlockSpec((tm, D), lambda m, n: (m, 0)),
                  pl.BlockSpec((1, D), lambda m, n: (0, 0))] + [w_spec] * len(ws) + c_in,
        out_specs=[pl.BlockSpec((tm, tn), lambda m, n: (m, n))] + c_out,
        out_shape=[jax.ShapeDtypeStruct((T, N), BF16)] + c_shapes,
        scratch_shapes=[pltpu.VMEM((tm, D), BF16)],
        compiler_params=pltpu.CompilerParams(
            dimension_semantics=("arbitrary", "arbitrary"),
            vmem_limit_bytes=_vmem_limit(est)),
        name="norm_glu" if len(ws) == 2 else "norm_inproj",
    )(h, gain.reshape(1, D), *ws, *c_ops)
    return outs[0], outs[1:]


def _mm_res_kernel(*refs, scale, n_cast):
    (a_ref, w_ref, h_ref), cast_src, o_ref, cast_dst, _ = _split_refs(refs, 3, n_cast)
    acc = _dot(a_ref[...], w_ref[...])
    o_ref[...] = h_ref[...] + scale * acc
    _do_casts(cast_src, cast_dst)


def _mm_res(a, w, h, scale, casts, *, tm, tn, in_place):
    T, K = a.shape
    N = w.shape[1]
    tm = _tile(T, tm)
    tn = _tile(N, tn)
    grid = (T // tm, N // tn)
    c_in, c_out, c_shapes, c_ops = _cast_plan(casts, grid)
    est = 2 * tm * K * 2 + 2 * K * tn * 2 + 4 * tm * tn * 4 + tm * tn * 4 + _cast_bytes(casts, grid)
    outs = pl.pallas_call(
        functools.partial(_mm_res_kernel, scale=scale, n_cast=len(casts)),
        grid=grid,
        in_specs=[pl.BlockSpec((tm, K), lambda m, n: (m, 0)),
                  pl.BlockSpec((K, tn), lambda m, n: (0, n)),
                  pl.BlockSpec((tm, tn), lambda m, n: (m, n))] + c_in,
        out_specs=[pl.BlockSpec((tm, tn), lambda m, n: (m, n))] + c_out,
        out_shape=[jax.ShapeDtypeStruct((T, N), F32)] + c_shapes,
        input_output_aliases={2: 0} if in_place else {},
        compiler_params=pltpu.CompilerParams(
            dimension_semantics=("arbitrary", "arbitrary"),
            vmem_limit_bytes=_vmem_limit(est)),
        name="mm_residual",
    )(a, w, h, *c_ops)
    return outs[0], outs[1:]


def _merge_kernel(*refs, n_cast):
    (ya_ref, yb_ref, wa_ref, wb_ref, ga_ref, gb_ref), cast_src, o_ref, cast_dst, _ = _split_refs(refs, 6, n_cast)
    a = _dot(ya_ref[...], wa_ref[...])
    b = _dot(yb_ref[...], wb_ref[...])
    ga = _sigmoid(ga_ref[...].astype(F32))
    gb = _sigmoid(gb_ref[...].astype(F32))
    o_ref[...] = (ga * a + gb * b).astype(o_ref.dtype)
    _do_casts(cast_src, cast_dst)


def _merge(ya, yb, wa, wb, z, ga_col, gb_col, casts, *, tm, tn):
    T, KA = ya.shape
    KB = yb.shape[1]
    N = wa.shape[1]
    tm = _tile(T, tm)
    tn = _tile(N, tn)
    grid = (T // tm, N // tn)
    ga_blk, gb_blk = ga_col // tn, gb_col // tn
    assert ga_blk * tn == ga_col and gb_blk * tn == gb_col
    c_in, c_out, c_shapes, c_ops = _cast_plan(casts, grid)
    est = (2 * tm * (KA + KB) * 2 + 2 * (KA + KB) * tn * 2 + 6 * tm * tn * 2 + 4 * tm * tn * 4
           + _cast_bytes(casts, grid))
    outs = pl.pallas_call(
        functools.partial(_merge_kernel, n_cast=len(casts)),
        grid=grid,
        in_specs=[pl.BlockSpec((tm, KA), lambda m, n: (m, 0)),
                  pl.BlockSpec((tm, KB), lambda m, n: (m, 0)),
                  pl.BlockSpec((KA, tn), lambda m, n: (0, n)),
                  pl.BlockSpec((KB, tn), lambda m, n: (0, n)),
                  pl.BlockSpec((tm, tn), lambda m, n: (m, ga_blk + n)),
                  pl.BlockSpec((tm, tn), lambda m, n: (m, gb_blk + n))] + c_in,
        out_specs=[pl.BlockSpec((tm, tn), lambda m, n: (m, n))] + c_out,
        out_shape=[jax.ShapeDtypeStruct((T, N), BF16)] + c_shapes,
        compiler_params=pltpu.CompilerParams(
            dimension_semantics=("arbitrary", "arbitrary"),
            vmem_limit_bytes=_vmem_limit(est)),
        name="gated_merge",
    )(ya, yb, wa, wb, z, z, *c_ops)
    return outs[0], outs[1:]


def _rmsnorm_kernel(h_ref, g_ref, o_ref):
    x = h_ref[...]
    ms = jnp.mean(x * x, axis=-1, keepdims=True)
    o_ref[...] = x * lax.rsqrt(ms + NORM_EPS) * g_ref[...]


def _rmsnorm(h, gain, *, tr):
    T, D = h.shape
    tr = min(tr, T)
    return pl.pallas_call(
        _rmsnorm_kernel,
        grid=(T // tr,),
        in_specs=[pl.BlockSpec((tr, D), lambda i: (i, 0)),
                  pl.BlockSpec((1, D), lambda i: (0, 0))],
        out_specs=pl.BlockSpec((tr, D), lambda i: (i, 0)),
        out_shape=jax.ShapeDtypeStruct((T, D), F32),
        compiler_params=pltpu.CompilerParams(dimension_semantics=("parallel",)),
        name="final_norm",
    )(h, gain.reshape(1, D))


def _rglru_kernel(x_ref, y_ref, cw_ref, cb_ref, wa_ref, ba_ref, wx_ref, bx_ref, lam_ref,
                  o_ref, xp_sc, carry_sc, *, conv_width):
    tt = x_ref.shape[0]
    n_heads, _, cb = wa_ref.shape
    pad = SUBLANES

    @pl.when(pl.program_id(2) == 0)
    def _():
        xp_sc[...] = jnp.zeros(xp_sc.shape, F32)
        carry_sc[...] = jnp.zeros(carry_sc.shape, F32)

    for hh in range(n_heads):
        cols = slice(hh * cb, (hh + 1) * cb)
        x = x_ref[:, cols].astype(F32)
        xpad = jnp.concatenate([xp_sc[hh], x], axis=0)
        cw = cw_ref[hh]
        xc = cb_ref[hh] + x * cw[conv_width - 1:conv_width, :]
        for s in range(1, conv_width):
            xc = xc + pltpu.roll(xpad, s, 0)[pad:] * cw[conv_width - 1 - s:conv_width - s, :]
        xp_sc[hh] = x[tt - pad:tt]

        xcb = xc.astype(BF16)
        r = _sigmoid(_dot(xcb, wa_ref[hh]) + ba_ref[hh])
        i = _sigmoid(_dot(xcb, wx_ref[hh]) + bx_ref[hh])
        log_a = (RG_C * jax.nn.log_sigmoid(lam_ref[hh])) * r
        a = jnp.exp(log_a)
        b = jnp.sqrt(-jnp.tanh(log_a) * (a * a + 1.0)) * (i * xc)

        a = a.reshape(tt // SUBLANES, SUBLANES, cb)
        b = b.reshape(tt // SUBLANES, SUBLANES, cb)
        row8 = lax.broadcasted_iota(jnp.int32, a.shape, 1)
        for d in (1, 2, 4):
            keep = row8 >= d
            a_prev = jnp.where(keep, pltpu.roll(a, d, 1), 1.0)
            b_prev = jnp.where(keep, pltpu.roll(b, d, 1), 0.0)
            b = a * b_prev + b
            a = a * a_prev
        a = a.reshape(tt, cb)
        b = b.reshape(tt, cb)
        carry = carry_sc[hh]
        hs = []
        for gi in range(tt // SUBLANES):
            g8 = slice(gi * SUBLANES, (gi + 1) * SUBLANES)
            hg = a[g8] * carry + b[g8]
            hs.append(hg)
            carry = jnp.broadcast_to(hg[SUBLANES - 1:SUBLANES, :], (SUBLANES, cb))
        carry_sc[hh] = carry
        hseq = jnp.concatenate(hs, axis=0)
        o_ref[:, cols] = (hseq * jax.nn.gelu(y_ref[:, cols].astype(F32))).astype(o_ref.dtype)


def _rglru(z, cw, cb, wa, ba, wx, bx, lam, *, batch, seq, d_rg, tt):
    heads, blk = wa.shape[0], wa.shape[1]
    hp = _tile(heads, RG_HEADS_PER_STEP)
    wb = hp * blk
    conv_width = cw.shape[0]
    tt = min(tt, seq)
    nt = seq // tt
    T = batch * seq
    vec = lambda v: v.reshape(heads, 1, blk)
    vspec = pl.BlockSpec((hp, 1, blk), lambda b, h, t: (h, 0, 0))
    wspec = pl.BlockSpec((hp, blk, blk), lambda b, h, t: (h, 0, 0))
    cwh = cw.reshape(conv_width, heads, blk).transpose(1, 0, 2)
    return pl.pallas_call(
        functools.partial(_rglru_kernel, conv_width=conv_width),
        grid=(batch, heads // hp, nt),
        in_specs=[pl.BlockSpec((tt, wb), lambda b, h, t: (b * nt + t, h)),
                  pl.BlockSpec((tt, wb), lambda b, h, t: (b * nt + t, heads // hp + h)),
                  pl.BlockSpec((hp, conv_width, blk), lambda b, h, t: (h, 0, 0)),
                  vspec, wspec, vspec, wspec, vspec, vspec],
        out_specs=pl.BlockSpec((tt, wb), lambda b, h, t: (b * nt + t, h)),
        out_shape=jax.ShapeDtypeStruct((T, d_rg), BF16),
        scratch_shapes=[pltpu.VMEM((hp, SUBLANES, blk), F32),
                        pltpu.VMEM((hp, SUBLANES, blk), F32)],
        compiler_params=pltpu.CompilerParams(
            dimension_semantics=("parallel", "parallel", "arbitrary")),
        name="rglru",
    )(z, z, cwh, vec(cb), wa, vec(ba), wx, vec(bx), vec(lam))


def _hg_constants(chunk):
    t = np.arange(chunk)[:, None]
    r = np.arange(chunk)[None, :]
    tril = (r <= t).astype(np.float32)
    return np.concatenate([tril, tril, tril], axis=1)


def _hg_masks(chunk):
    t = np.arange(chunk)[:, None]
    s = np.arange(chunk)[None, :]
    masks = [(t == s)]
    m = chunk // 4
    while m >= 1:
        masks.append(((t ^ s) >> int(math.log2(m))) == 1)
        m //= 2
    return np.stack(masks).astype(np.float32)


def _split3(x):
    hi = x.astype(BF16)
    r1 = x - hi.astype(F32)
    mid = r1.astype(BF16)
    lo = (r1 - mid.astype(F32)).astype(BF16)
    return hi, mid, lo


def _hg_level_operands(q, kk, fd, b2, chunk):
    dk = q.shape[1]
    zeros = lambda n: jnp.zeros((n, dk), F32)
    qes, kes, halves = [q], [kk], [0]
    m = chunk // 2
    while m >= SUBLANES:
        qparts, kparts = [], []
        for s in range(0, chunk, 2 * m):
            ref = b2[s + m - 1:s + m, :]
            qparts += [zeros(m), q[s + m:s + 2 * m] * jnp.exp2(b2[s + m:s + 2 * m] - ref)]
            kparts += [kk[s:s + m] * jnp.exp2(ref - b2[s:s + m]), zeros(m)]
        qes.append(jnp.concatenate(qparts, axis=0))
        kes.append(jnp.concatenate(kparts, axis=0))
        halves.append(m)
        m //= 2
    row = lax.broadcasted_iota(jnp.int32, (chunk, dk), 0)
    b3 = b2.reshape(chunk // SUBLANES, SUBLANES, dk)
    ref = jnp.broadcast_to(b3[:, 3:4, :], b3.shape).reshape(chunk, dk)
    w = jnp.exp2(-jnp.abs(b2 - ref))
    upper = (row & 4) != 0
    qes.append(jnp.where(upper, q * w, 0.0))
    kes.append(jnp.where(upper, 0.0, kk * w))
    halves.append(4)
    f3 = fd.reshape(chunk // SUBLANES, SUBLANES, dk)
    f_prev = pltpu.roll(f3, 1, 1).reshape(chunk, dk)
    f_next = pltpu.roll(f3, SUBLANES - 1, 1).reshape(chunk, dk)
    r4 = row & 3
    qes.append(jnp.where(r4 >= 2, q * jnp.where(r4 == 3, fd * f_prev, fd), 0.0))
    kes.append(jnp.where(r4 >= 2, 0.0, kk * jnp.where(r4 == 0, f_next, 1.0)))
    halves.append(2)
    odd = (row & 1) != 0
    qes.append(jnp.where(odd, q * fd, 0.0))
    kes.append(jnp.where(odd, 0.0, kk))
    halves.append(1)
    return qes, kes, halves


def _hg_core(zq, zf, v, zg, lbc, gain, mc, pm_ref, st, chunk):
    rows, dk = zq.shape
    nch = rows // chunk
    log_lb, log_1m_lb, one_m_lb, lb = lbc[0:1], lbc[1:2], lbc[2:3], lbc[3:4]
    zf = zf.astype(F32)
    e = jnp.exp2(jnp.abs(zf) * (-LOG2E))
    den = 1.0 + e
    inv = 1.0 / den
    t = one_m_lb * jnp.where(zf >= 0.0, inv, e * inv)
    fd = lb + t
    kk = one_m_lb - t
    x2 = log_1m_lb + (jnp.minimum(zf, 0.0) - jnp.log(den))
    lf2 = (jnp.maximum(log_lb, x2)
           + jnp.log(1.0 + jnp.exp2(jnp.abs(log_lb - x2) * (-LOG2E)))) * LOG2E
    q = _silu(zq.astype(F32))

    w3 = jnp.concatenate(
        [jnp.concatenate([p[c * chunk:(c + 1) * chunk] for c in range(nch)], axis=1)
         for p in _split3(lf2)], axis=0)
    xw = _dot(mc, w3)

    ps, qds, upds, decs = [], [], [], []
    for c in range(nch):
        sl = slice(c * chunk, (c + 1) * chunk)
        b2 = xw[:, c * dk:(c + 1) * dk]
        qc, kc = q[sl], kk[sl]
        qes, kes, halves = _hg_level_operands(qc, kc, fd[sl], b2, chunk)
        p = _dot_dims(qes[1].astype(BF16), kes[1].astype(BF16), NT_DIMS)
        for lv in [0] + list(range(2, len(halves))):
            s = _dot_dims(qes[lv].astype(BF16), kes[lv].astype(BF16), NT_DIMS)
            p = p + s * pm_ref[0 if lv == 0 else lv - 1]
        ps.append(p.astype(BF16))
        qds.append((qc * jnp.exp2(b2)).astype(BF16))
        b_last = b2[chunk - 1:chunk, :]
        k_dec = (kc * jnp.exp2(b_last - b2)).astype(BF16)
        upds.append(_dot_dims(v[sl], k_dec, TN_DIMS))
        decs.append(jnp.exp2(b_last))

    outs = []
    for c in range(nch):
        sl = slice(c * chunk, (c + 1) * chunk)
        outs.append(_dot(ps[c], v[sl]) + _dot_dims(qds[c], st.astype(BF16), NT_DIMS))
        st = st * decs[c] + upds[c]

    o = jnp.concatenate(outs, axis=0)
    o = o * lax.rsqrt(jnp.mean(o * o, axis=-1, keepdims=True) + NORM_EPS) * gain
    return o * _silu(zg.astype(F32)), st


def _hgrn2_kernel(q_ref, f_ref, i_ref, g_ref, lbc_ref, gain_ref, mc_ref, pm_ref, o_ref, st_sc, *, chunk, sub):
    tt = q_ref.shape[0]
    dk = HG_HEAD_DIM
    n_heads = st_sc.shape[0]

    @pl.when(pl.program_id(2) == 0)
    def _():
        st_sc[...] = jnp.zeros(st_sc.shape, F32)

    def body(j, carry):
        rows = pl.ds(pl.multiple_of(j * sub, sub), sub)
        for hh in range(n_heads):
            cols = slice(hh * dk, (hh + 1) * dk)
            y, st = _hg_core(q_ref[rows, cols], f_ref[rows, cols], i_ref[rows, cols], g_ref[rows, cols],
                             lbc_ref[hh], gain_ref[hh], mc_ref[...], pm_ref, st_sc[hh], chunk)
            st_sc[hh] = st
            o_ref[rows, cols] = y.astype(o_ref.dtype)
        return carry

    lax.fori_loop(0, tt // sub, body, 0)


def _hgrn2(z, lbc, gain, *, batch, seq, d_hg, col0, tt, sub, chunk):
    dk = HG_HEAD_DIM
    heads = d_hg // dk
    hp = _tile(heads, HG_HEADS_PER_STEP)
    wb = hp * dk
    tt = min(tt, seq)
    sub = min(sub, tt)
    chunk = min(chunk, sub)
    assert chunk >= 2 * SUBLANES and tt % sub == 0 and sub % chunk == 0
    nt = seq // tt
    T = batch * seq
    mc = jnp.asarray(_hg_constants(chunk), BF16)
    pm = jnp.asarray(_hg_masks(chunk), F32)
    assert col0 % wb == 0 and d_hg % wb == 0

    def zspec(j):
        return pl.BlockSpec((tt, wb), lambda b, h, t: (b * nt + t, (col0 + j * d_hg) // wb + h))

    return pl.pallas_call(
        functools.partial(_hgrn2_kernel, chunk=chunk, sub=sub),
        grid=(batch, heads // hp, nt),
        in_specs=[zspec(0), zspec(1), zspec(2), zspec(3),
                  pl.BlockSpec((hp, 4, dk), lambda b, h, t: (h, 0, 0)),
                  pl.BlockSpec((hp, 1, dk), lambda b, h, t: (h, 0, 0)),
                  pl.BlockSpec(mc.shape, lambda b, h, t: (0, 0)),
                  pl.BlockSpec(pm.shape, lambda b, h, t: (0, 0, 0))],
        out_specs=pl.BlockSpec((tt, wb), lambda b, h, t: (b * nt + t, h)),
        out_shape=jax.ShapeDtypeStruct((T, d_hg), BF16),
        scratch_shapes=[pltpu.VMEM((hp, dk, dk), F32)],
        compiler_params=pltpu.CompilerParams(
            dimension_semantics=("parallel", "parallel", "arbitrary")),
        name="hgrn2",
    )(z, z, z, z, lbc, gain.reshape(heads, 1, dk), mc, pm)


def _lb_kernel(p_ref, o_ref):
    depth = p_ref.shape[0]
    p = p_ref[...]
    mx = jnp.max(p, axis=0, keepdims=True)
    ex = jnp.exp(p - mx)
    sm = ex / jnp.sum(ex, axis=0, keepdims=True)
    c0 = sm[0:1, :]
    c = c0
    for l in range(depth):
        if l > 0:
            c = c + sm[l:l + 1, :]
        lb = c - c0
        o_ref[l, 0:1, :] = jnp.log(lb)
        o_ref[l, 1:2, :] = jnp.log1p(-lb)
        o_ref[l, 2:3, :] = 1.0 - lb
        o_ref[l, 3:4, :] = lb


def _lb_constants(lb_param):
    depth, d_hg = lb_param.shape
    return pl.pallas_call(
        _lb_kernel,
        out_shape=jax.ShapeDtypeStruct((depth, 4, d_hg), F32),
        name="hgrn2_lower_bounds",
    )(lb_param)


def kernel(x, ffn1_norm, ffn1_w_gate, ffn1_w_up, ffn1_w_down, mix_norm, w_in, conv_w, conv_b, rg_wa, rg_ba, rg_wx, rg_bx, rg_lambda, hg_lower_bounds, hg_out_norm, w_out_a, w_out_b, w_out, ffn2_norm, ffn2_w_gate, ffn2_w_up, ffn2_w_down, final_norm):
    batch, seq, d_model = x.shape
    depth = w_in.shape[0]
    d_rg = rg_lambda.shape[1]
    d_hg = hg_lower_bounds.shape[1]
    heads_hg = d_hg // HG_HEAD_DIM
    T = batch * seq
    tm, tn = ROW_TILE, COL_TILE

    lbc = _lb_constants(hg_lower_bounds)
    lbc = lbc.reshape(depth, 4, heads_hg, HG_HEAD_DIM).transpose(0, 2, 1, 3)

    bf = lambda w: w.astype(BF16)
    names = ("wg1", "wu1", "wd1", "w_in", "wa", "wb", "w_out", "wg2", "wu2", "wd2")
    stacks = dict(zip(names, (ffn1_w_gate, ffn1_w_up, ffn1_w_down, w_in, w_out_a, w_out_b, w_out,
                              ffn2_w_gate, ffn2_w_up, ffn2_w_down)))
    wt = {(k, 0): bf(stacks[k][0]) for k in ("wg1", "wu1", "wd1")}
    rg_wa16, rg_wx16 = bf(rg_wa), bf(rg_wx)

    def run(fn, to_cast, *args, **kw):
        todo = [(k, ll) for k, ll in to_cast if ll < depth and (k, ll) not in wt]
        out, cast = fn(*args, [(stacks[k], ll) for k, ll in todo], **kw)
        wt.update(zip(todo, cast))
        return out

    h = x.reshape(T, d_model)
    for l in range(depth):
        early = [("w_in", l), ("wa", l), ("wb", l)]
        act = run(_norm_mm, [("wg1", l + 1), ("wu1", l + 1)] + early,
                  h, ffn1_norm[l], [wt["wg1", l], wt["wu1", l]], tm=tm // 2, tn=tn)
        early = [("w_out", l), ("wg2", l), ("wu2", l), ("wd2", l)]
        h = run(_mm_res, [("wd1", l + 1)] + early,
                act, wt["wd1", l], h, FFN_RES_WEIGHT, tm=tm, tn=tn, in_place=l > 0)

        z = run(_norm_mm, [("w_in", l + 1)], h, mix_norm[l], [wt["w_in", l]], tm=tm // 2, tn=2 * tn)
        y_a = _rglru(z, conv_w[l], conv_b[l], rg_wa16[l], rg_ba[l], rg_wx16[l], rg_bx[l], rg_lambda[l],
                     batch=batch, seq=seq, d_rg=d_rg, tt=SEQ_TILE)
        y_b = _hgrn2(z, lbc[l], hg_out_norm[l], batch=batch, seq=seq, d_hg=d_hg,
                     col0=2 * d_rg, tt=HG_SEQ_TILE, sub=HG_SUB, chunk=HG_CHUNK)
        ga_col = 2 * d_rg + 4 * d_hg
        merged = run(_merge, [("wa", l + 1), ("wb", l + 1)], y_a, y_b, wt["wa", l], wt["wb", l], z,
                     ga_col, ga_col + d_model, tm=tm, tn=tn)
        h = run(_mm_res, [("w_out", l + 1)], merged, wt["w_out", l], h, 1.0, tm=tm, tn=tn, in_place=True)

        act = run(_norm_mm, [("wg2", l + 1), ("wu2", l + 1)],
                  h, ffn2_norm[l], [wt["wg2", l], wt["wu2", l]], tm=tm // 2, tn=tn)
        h = run(_mm_res, [("wd2", l + 1)], act, wt["wd2", l], h, FFN_RES_WEIGHT, tm=tm, tn=tn, in_place=True)
    out = _rmsnorm(h, final_norm, tr=256)
    return out.reshape(batch, seq, d_model)
```

```python
import functools
import math

import numpy as np
import jax
import jax.numpy as jnp
from jax import lax
from jax.experimental import pallas as pl
from jax.experimental.pallas import tpu as pltpu

F32 = jnp.float32
BF16 = jnp.bfloat16

NORM_EPS = 1e-6
FFN_RES_WEIGHT = 0.5
RG_C = 8.0
HG_HEAD_DIM = 128

V7X_VMEM_BYTES = 64 * 1024 * 1024
SUBLANES = 8
LANES = 128
BF16_ROWS = 16

ROW_TILE = 1024
COL_TILE = 512
NORM_ROWS = 128
SEQ_TILE = 512
HG_SEQ_TILE = 1024
HG_SUB = 512
HG_CHUNK = 64
HG_HEADS_PER_STEP = 4
RG_HEADS_PER_STEP = 2

LOG2E = 1.4426950408889634
NT_DIMS = (((1,), (1,)), ((), ()))
TN_DIMS = (((0,), (0,)), ((), ()))


def _vmem_limit(block_bytes):
    return int(min(V7X_VMEM_BYTES - 4 * 1024 * 1024, block_bytes * 1.3 + 8 * 1024 * 1024))


def _tile(n, pref):
    t = min(pref, n)
    while n % t:
        t //= 2
    return t


def _dot(a, b):
    return jnp.dot(a, b, preferred_element_type=F32)


def _dot_dims(a, b, dims):
    return lax.dot_general(a, b, dims, preferred_element_type=F32)


def _sigmoid(x):
    return 1.0 / (1.0 + jnp.exp2(x * (-LOG2E)))


def _silu(x):
    return x * _sigmoid(x)


def _norm_rows(h_ref, g_ref, u_ref, rows):
    D = h_ref.shape[1]
    ssq = None
    for c in range(0, D, D // 2):
        x = h_ref[rows, c:c + D // 2]
        part = jnp.sum(x * x, axis=-1, keepdims=True)
        ssq = part if ssq is None else ssq + part
    scale = lax.rsqrt(ssq * (1.0 / D) + NORM_EPS)
    for c in range(0, D, D // 4):
        cols = slice(c, c + D // 4)
        u_ref[rows, cols] = (h_ref[rows, cols] * scale * g_ref[:, cols]).astype(BF16)


def _cast_rows(rows, steps):
    pr = BF16_ROWS
    while rows % pr or rows // pr > steps:
        pr += BF16_ROWS
    return pr


def _cast_plan(casts, grid):
    steps = grid[0] * grid[1]
    in_specs, out_specs, out_shapes, operands = [], [], [], []
    for stack, layer in casts:
        _, R, C = stack.shape
        pr = _cast_rows(R, steps)
        pieces = R // pr

        def piece(m, n, pieces=pieces):
            return jnp.minimum(m * grid[1] + n, pieces - 1)

        in_specs.append(pl.BlockSpec((None, pr, C), lambda m, n, layer=layer, piece=piece: (layer, piece(m, n), 0)))
        out_specs.append(pl.BlockSpec((pr, C), lambda m, n, piece=piece: (piece(m, n), 0)))
        out_shapes.append(jax.ShapeDtypeStruct((R, C), BF16))
        operands.append(stack)
    return in_specs, out_specs, out_shapes, operands


def _cast_bytes(casts, grid):
    steps = grid[0] * grid[1]
    total = 0
    for stack, _ in casts:
        _, R, C = stack.shape
        total += 2 * _cast_rows(R, steps) * C * (4 + 2)
    return total


def _do_casts(src_refs, dst_refs):
    for s, d in zip(src_refs, dst_refs):
        d[...] = s[...].astype(BF16)


def _split_refs(refs, n_in, n_cast):
    ins = refs[:n_in]
    cast_src = refs[n_in:n_in + n_cast]
    out = refs[n_in + n_cast]
    cast_dst = refs[n_in + n_cast + 1:n_in + 2 * n_cast + 1]
    scratch = refs[n_in + 2 * n_cast + 1:]
    return ins, cast_src, out, cast_dst, scratch


def _norm_mm_kernel(*refs, n_w, n_cast):
    (h_ref, g_ref, *w_refs), cast_src, o_ref, cast_dst, (u_sc,) = _split_refs(refs, 2 + n_w, n_cast)

    @pl.when(pl.program_id(1) == 0)
    def _():
        rows = h_ref.shape[0]
        step = min(NORM_ROWS, rows)

        def body(i, c):
            _norm_rows(h_ref, g_ref, u_sc, pl.ds(pl.multiple_of(i * step, step), step))
            return c

        lax.fori_loop(0, rows // step, body, 0)

    u = u_sc[...]
    if n_w == 2:
        g = _dot(u, w_refs[0][...])
        up = _dot(u, w_refs[1][...])
        o_ref[...] = (_silu(g) * up).astype(o_ref.dtype)
    else:
        o_ref[...] = _dot(u, w_refs[0][...]).astype(o_ref.dtype)
    _do_casts(cast_src, cast_dst)


def _norm_mm(h, gain, ws, casts, *, tm, tn):
    T, D = h.shape
    N = ws[0].shape[1]
    tm = _tile(T, tm)
    tn = _tile(N, tn)
    grid = (T // tm, N // tn)
    w_spec = pl.BlockSpec((D, tn), lambda m, n: (0, n))
    c_in, c_out, c_shapes, c_ops = _cast_plan(casts, grid)
    est = (2 * tm * D * 4 + tm * D * 2 + len(ws) * 2 * D * tn * 2 + 2 * tm * tn * 2
           + (1 + len(ws)) * tm * tn * 4 + _cast_bytes(casts, grid))
    outs = pl.pallas_call(
        functools.partial(_norm_mm_kernel, n_w=len(ws), n_cast=len(casts)),
        grid=grid,
        in_specs=[pl.BlockSpec((tm, D), lambda m, n: (m, 0)),
                  pl.BlockSpec((1, D), lambda m, n: (0, 0))] + [w_spec] * len(ws) + c_in,
        out_specs=[pl.BlockSpec((tm, tn), lambda m, n: (m, n))] + c_out,
        out_shape=[jax.ShapeDtypeStruct((T, N), BF16)] + c_shapes,
        scratch_shapes=[pltpu.VMEM((tm, D), BF16)],
        compiler_params=pltpu.CompilerParams(
            dimension_semantics=("arbitrary", "arbitrary"),
            vmem_limit_bytes=_vmem_limit(est)),
        name="norm_glu" if len(ws) == 2 else "norm_inproj",
    )(h, gain.reshape(1, D), *ws, *c_ops)
    return outs[0], outs[1:]


def _mm_res_kernel(*refs, scale, n_cast):
    (a_ref, w_ref, h_ref), cast_src, o_ref, cast_dst, _ = _split_refs(refs, 3, n_cast)
    acc = _dot(a_ref[...], w_ref[...])
    o_ref[...] = h_ref[...] + scale * acc
    _do_casts(cast_src, cast_dst)


def _mm_res(a, w, h, scale, casts, *, tm, tn, in_place):
    T, K = a.shape
    N = w.shape[1]
    tm = _tile(T, tm)
    tn = _tile(N, tn)
    grid = (N // tn, T // tm)
    c_in, c_out, c_shapes, c_ops = _cast_plan(casts, grid)
    est = 2 * tm * K * 2 + 2 * K * tn * 2 + 4 * tm * tn * 4 + tm * tn * 4 + _cast_bytes(casts, grid)
    outs = pl.pallas_call(
        functools.partial(_mm_res_kernel, scale=scale, n_cast=len(casts)),
        grid=grid,
        in_specs=[pl.BlockSpec((tm, K), lambda n, m: (m, 0)),
                  pl.BlockSpec((K, tn), lambda n, m: (0, n)),
                  pl.BlockSpec((tm, tn), lambda n, m: (m, n))] + c_in,
        out_specs=[pl.BlockSpec((tm, tn), lambda n, m: (m, n))] + c_out,
        out_shape=[jax.ShapeDtypeStruct((T, N), F32)] + c_shapes,
        input_output_aliases={2: 0} if in_place else {},
        compiler_params=pltpu.CompilerParams(
            dimension_semantics=("arbitrary", "arbitrary"),
            vmem_limit_bytes=_vmem_limit(est)),
        name="mm_residual",
    )(a, w, h, *c_ops)
    return outs[0], outs[1:]


def _merge_kernel(*refs, n_cast):
    (ya_ref, yb_ref, wa_ref, wb_ref, ga_ref, gb_ref), cast_src, o_ref, cast_dst, _ = _split_refs(refs, 6, n_cast)
    a = _dot(ya_ref[...], wa_ref[...])
    b = _dot(yb_ref[...], wb_ref[...])
    ga = _sigmoid(ga_ref[...].astype(F32))
    gb = _sigmoid(gb_ref[...].astype(F32))
    o_ref[...] = (ga * a + gb * b).astype(o_ref.dtype)
    _do_casts(cast_src, cast_dst)


def _merge(ya, yb, wa, wb, z, ga_col, gb_col, casts, *, tm, tn):
    T, KA = ya.shape
    KB = yb.shape[1]
    N = wa.shape[1]
    tm = _tile(T, tm)
    tn = _tile(N, tn)
    grid = (T // tm, N // tn)
    ga_blk, gb_blk = ga_col // tn, gb_col // tn
    assert ga_blk * tn == ga_col and gb_blk * tn == gb_col
    c_in, c_out, c_shapes, c_ops = _cast_plan(casts, grid)
    est = (2 * tm * (KA + KB) * 2 + 2 * (KA + KB) * tn * 2 + 6 * tm * tn * 2 + 4 * tm * tn * 4
           + _cast_bytes(casts, grid))
    outs = pl.pallas_call(
        functools.partial(_merge_kernel, n_cast=len(casts)),
        grid=grid,
        in_specs=[pl.BlockSpec((tm, KA), lambda m, n: (m, 0)),
                  pl.BlockSpec((tm, KB), lambda m, n: (m, 0)),
                  pl.BlockSpec((KA, tn), lambda m, n: (0, n)),
                  pl.BlockSpec((KB, tn), lambda m, n: (0, n)),
                  pl.BlockSpec((tm, tn), lambda m, n: (m, ga_blk + n)),
                  pl.BlockSpec((tm, tn), lambda m, n: (m, gb_blk + n))] + c_in,
        out_specs=[pl.BlockSpec((tm, tn), lambda m, n: (m, n))] + c_out,
        out_shape=[jax.ShapeDtypeStruct((T, N), BF16)] + c_shapes,
        compiler_params=pltpu.CompilerParams(
            dimension_semantics=("arbitrary", "arbitrary"),
            vmem_limit_bytes=_vmem_limit(est)),
        name="gated_merge",
    )(ya, yb, wa, wb, z, z, *c_ops)
    return outs[0], outs[1:]


def _rmsnorm_kernel(h_ref, g_ref, o_ref):
    x = h_ref[...]
    ms = jnp.mean(x * x, axis=-1, keepdims=True)
    o_ref[...] = x * lax.rsqrt(ms + NORM_EPS) * g_ref[...]


def _rmsnorm(h, gain, *, tr):
    T, D = h.shape
    tr = min(tr, T)
    return pl.pallas_call(
        _rmsnorm_kernel,
        grid=(T // tr,),
        in_specs=[pl.BlockSpec((tr, D), lambda i: (i, 0)),
                  pl.BlockSpec((1, D), lambda i: (0, 0))],
        out_specs=pl.BlockSpec((tr, D), lambda i: (i, 0)),
        out_shape=jax.ShapeDtypeStruct((T, D), F32),
        compiler_params=pltpu.CompilerParams(dimension_semantics=("parallel",)),
        name="final_norm",
    )(h, gain.reshape(1, D))


def _rglru_kernel(x_ref, y_ref, cw_ref, cb_ref, wa_ref, ba_ref, wx_ref, bx_ref, lam_ref,
                  o_ref, xp_sc, carry_sc, *, conv_width):
    tt = x_ref.shape[0]
    n_heads, _, cb = wa_ref.shape
    pad = SUBLANES

    @pl.when(pl.program_id(2) == 0)
    def _():
        xp_sc[...] = jnp.zeros(xp_sc.shape, F32)
        carry_sc[...] = jnp.zeros(carry_sc.shape, F32)

    for hh in range(n_heads):
        cols = slice(hh * cb, (hh + 1) * cb)
        x = x_ref[:, cols].astype(F32)
        xpad = jnp.concatenate([xp_sc[hh], x], axis=0)
        cw = cw_ref[hh]
        xc = cb_ref[hh] + x * cw[conv_width - 1:conv_width, :]
        for s in range(1, conv_width):
            xc = xc + pltpu.roll(xpad, s, 0)[pad:] * cw[conv_width - 1 - s:conv_width - s, :]
        xp_sc[hh] = x[tt - pad:tt]

        xcb = xc.astype(BF16)
        r = _sigmoid(_dot(xcb, wa_ref[hh]) + ba_ref[hh])
        i = _sigmoid(_dot(xcb, wx_ref[hh]) + bx_ref[hh])
        log_a = (RG_C * jax.nn.log_sigmoid(lam_ref[hh])) * r
        a = jnp.exp(log_a)
        b = jnp.sqrt(-jnp.tanh(log_a) * (a * a + 1.0)) * (i * xc)

        a = a.reshape(tt // SUBLANES, SUBLANES, cb)
        b = b.reshape(tt // SUBLANES, SUBLANES, cb)
        row8 = lax.broadcasted_iota(jnp.int32, a.shape, 1)
        for d in (1, 2, 4):
            keep = row8 >= d
            a_prev = jnp.where(keep, pltpu.roll(a, d, 1), 1.0)
            b_prev = jnp.where(keep, pltpu.roll(b, d, 1), 0.0)
            b = a * b_prev + b
            a = a * a_prev
        a = a.reshape(tt, cb)
        b = b.reshape(tt, cb)
        carry = carry_sc[hh]
        hs = []
        for gi in range(tt // SUBLANES):
            g8 = slice(gi * SUBLANES, (gi + 1) * SUBLANES)
            hg = a[g8] * carry + b[g8]
            hs.append(hg)
            carry = jnp.broadcast_to(hg[SUBLANES - 1:SUBLANES, :], (SUBLANES, cb))
        carry_sc[hh] = carry
        hseq = jnp.concatenate(hs, axis=0)
        o_ref[:, cols] = (hseq * jax.nn.gelu(y_ref[:, cols].astype(F32))).astype(o_ref.dtype)


def _rglru(z, cw, cb, wa, ba, wx, bx, lam, *, batch, seq, d_rg, tt):
    heads, blk = wa.shape[0], wa.shape[1]
    hp = _tile(heads, RG_HEADS_PER_STEP)
    wb = hp * blk
    conv_width = cw.shape[0]
    tt = min(tt, seq)
    nt = seq // tt
    T = batch * seq
    vec = lambda v: v.reshape(heads, 1, blk)
    vspec = pl.BlockSpec((hp, 1, blk), lambda b, h, t: (h, 0, 0))
    wspec = pl.BlockSpec((hp, blk, blk), lambda b, h, t: (h, 0, 0))
    cwh = cw.reshape(conv_width, heads, blk).transpose(1, 0, 2)
    return pl.pallas_call(
        functools.partial(_rglru_kernel, conv_width=conv_width),
        grid=(batch, heads // hp, nt),
        in_specs=[pl.BlockSpec((tt, wb), lambda b, h, t: (b * nt + t, h)),
                  pl.BlockSpec((tt, wb), lambda b, h, t: (b * nt + t, heads // hp + h)),
                  pl.BlockSpec((hp, conv_width, blk), lambda b, h, t: (h, 0, 0)),
                  vspec, wspec, vspec, wspec, vspec, vspec],
        out_specs=pl.BlockSpec((tt, wb), lambda b, h, t: (b * nt + t, h)),
        out_shape=jax.ShapeDtypeStruct((T, d_rg), BF16),
        scratch_shapes=[pltpu.VMEM((hp, SUBLANES, blk), F32),
                        pltpu.VMEM((hp, SUBLANES, blk), F32)],
        compiler_params=pltpu.CompilerParams(
            dimension_semantics=("parallel", "parallel", "arbitrary")),
        name="rglru",
    )(z, z, cwh, vec(cb), wa, vec(ba), wx, vec(bx), vec(lam))


def _hg_constants(chunk):
    t = np.arange(chunk)[:, None]
    r = np.arange(chunk)[None, :]
    tril = (r <= t).astype(np.float32)
    return np.concatenate([tril, tril, tril], axis=1)


def _hg_masks(chunk):
    t = np.arange(chunk)[:, None]
    s = np.arange(chunk)[None, :]
    masks = [(t == s)]
    m = chunk // 4
    while m >= 1:
        masks.append(((t ^ s) >> int(math.log2(m))) == 1)
        m //= 2
    return np.stack(masks).astype(np.float32)


def _split3(x):
    hi = x.astype(BF16)
    r1 = x - hi.astype(F32)
    mid = r1.astype(BF16)
    lo = (r1 - mid.astype(F32)).astype(BF16)
    return hi, mid, lo


def _hg_level_operands(q, kk, fd, b2, chunk):
    dk = q.shape[1]
    zeros = lambda n: jnp.zeros((n, dk), F32)
    qes, kes, halves = [q], [kk], [0]
    m = chunk // 2
    while m >= SUBLANES:
        qparts, kparts = [], []
        for s in range(0, chunk, 2 * m):
            ref = b2[s + m - 1:s + m, :]
            qparts += [zeros(m), q[s + m:s + 2 * m] * jnp.exp2(b2[s + m:s + 2 * m] - ref)]
            kparts += [kk[s:s + m] * jnp.exp2(ref - b2[s:s + m]), zeros(m)]
        qes.append(jnp.concatenate(qparts, axis=0))
        kes.append(jnp.concatenate(kparts, axis=0))
        halves.append(m)
        m //= 2
    row = lax.broadcasted_iota(jnp.int32, (chunk, dk), 0)
    b3 = b2.reshape(chunk // SUBLANES, SUBLANES, dk)
    ref = jnp.broadcast_to(b3[:, 3:4, :], b3.shape).reshape(chunk, dk)
    w = jnp.exp2(-jnp.abs(b2 - ref))
    upper = (row & 4) != 0
    qes.append(jnp.where(upper, q * w, 0.0))
    kes.append(jnp.where(upper, 0.0, kk * w))
    halves.append(4)
    f3 = fd.reshape(chunk // SUBLANES, SUBLANES, dk)
    f_prev = pltpu.roll(f3, 1, 1).reshape(chunk, dk)
    f_next = pltpu.roll(f3, SUBLANES - 1, 1).reshape(chunk, dk)
    r4 = row & 3
    qes.append(jnp.where(r4 >= 2, q * jnp.where(r4 == 3, fd * f_prev, fd), 0.0))
    kes.append(jnp.where(r4 >= 2, 0.0, kk * jnp.where(r4 == 0, f_next, 1.0)))
    halves.append(2)
    odd = (row & 1) != 0
    qes.append(jnp.where(odd, q * fd, 0.0))
    kes.append(jnp.where(odd, 0.0, kk))
    halves.append(1)
    return qes, kes, halves


def _hg_core(zq, zf, v, zg, lbc, gain, mc, pm_ref, st, chunk):
    rows, dk = zq.shape
    nch = rows // chunk
    log_lb, log_1m_lb, one_m_lb, lb = lbc[0:1], lbc[1:2], lbc[2:3], lbc[3:4]
    zf = zf.astype(F32)
    e = jnp.exp2(jnp.abs(zf) * (-LOG2E))
    den = 1.0 + e
    inv = 1.0 / den
    t = one_m_lb * jnp.where(zf >= 0.0, inv, e * inv)
    fd = lb + t
    kk = one_m_lb - t
    x2 = log_1m_lb + (jnp.minimum(zf, 0.0) - jnp.log(den))
    lf2 = (jnp.maximum(log_lb, x2)
           + jnp.log(1.0 + jnp.exp2(jnp.abs(log_lb - x2) * (-LOG2E)))) * LOG2E
    q = _silu(zq.astype(F32))

    w3 = jnp.concatenate(
        [jnp.concatenate([p[c * chunk:(c + 1) * chunk] for c in range(nch)], axis=1)
         for p in _split3(lf2)], axis=0)
    xw = _dot(mc, w3)

    ps, qds, upds, decs = [], [], [], []
    for c in range(nch):
        sl = slice(c * chunk, (c + 1) * chunk)
        b2 = xw[:, c * dk:(c + 1) * dk]
        qc, kc = q[sl], kk[sl]
        qes, kes, halves = _hg_level_operands(qc, kc, fd[sl], b2, chunk)
        p = _dot_dims(qes[1].astype(BF16), kes[1].astype(BF16), NT_DIMS)
        for lv in [0] + list(range(2, len(halves))):
            s = _dot_dims(qes[lv].astype(BF16), kes[lv].astype(BF16), NT_DIMS)
            p = p + s * pm_ref[0 if lv == 0 else lv - 1]
        ps.append(p.astype(BF16))
        qds.append((qc * jnp.exp2(b2)).astype(BF16))
        b_last = b2[chunk - 1:chunk, :]
        k_dec = (kc * jnp.exp2(b_last - b2)).astype(BF16)
        upds.append(_dot_dims(v[sl], k_dec, TN_DIMS))
        decs.append(jnp.exp2(b_last))

    outs = []
    for c in range(nch):
        sl = slice(c * chunk, (c + 1) * chunk)
        outs.append(_dot(ps[c], v[sl]) + _dot_dims(qds[c], st.astype(BF16), NT_DIMS))
        st = st * decs[c] + upds[c]

    o = jnp.concatenate(outs, axis=0)
    o = o * lax.rsqrt(jnp.mean(o * o, axis=-1, keepdims=True) + NORM_EPS) * gain
    return o * _silu(zg.astype(F32)), st


def _hgrn2_kernel(q_ref, f_ref, i_ref, g_ref, lbc_ref, gain_ref, mc_ref, pm_ref, o_ref, st_sc, *, chunk, sub):
    tt = q_ref.shape[0]
    dk = HG_HEAD_DIM
    n_heads = st_sc.shape[0]

    @pl.when(pl.program_id(2) == 0)
    def _():
        st_sc[...] = jnp.zeros(st_sc.shape, F32)

    def body(j, carry):
        rows = pl.ds(pl.multiple_of(j * sub, sub), sub)
        for hh in range(n_heads):
            cols = slice(hh * dk, (hh + 1) * dk)
            y, st = _hg_core(q_ref[rows, cols], f_ref[rows, cols], i_ref[rows, cols], g_ref[rows, cols],
                             lbc_ref[hh], gain_ref[hh], mc_ref[...], pm_ref, st_sc[hh], chunk)
            st_sc[hh] = st
            o_ref[rows, cols] = y.astype(o_ref.dtype)
        return carry

    lax.fori_loop(0, tt // sub, body, 0)


def _hgrn2(z, lbc, gain, *, batch, seq, d_hg, col0, tt, sub, chunk):
    dk = HG_HEAD_DIM
    heads = d_hg // dk
    hp = _tile(heads, HG_HEADS_PER_STEP)
    wb = hp * dk
    tt = min(tt, seq)
    sub = min(sub, tt)
    chunk = min(chunk, sub)
    assert chunk >= 2 * SUBLANES and tt % sub == 0 and sub % chunk == 0
    nt = seq // tt
    T = batch * seq
    mc = jnp.asarray(_hg_constants(chunk), BF16)
    pm = jnp.asarray(_hg_masks(chunk), F32)
    assert col0 % wb == 0 and d_hg % wb == 0

    def zspec(j):
        return pl.BlockSpec((tt, wb), lambda b, h, t: (b * nt + t, (col0 + j * d_hg) // wb + h))

    return pl.pallas_call(
        functools.partial(_hgrn2_kernel, chunk=chunk, sub=sub),
        grid=(batch, heads // hp, nt),
        in_specs=[zspec(0), zspec(1), zspec(2), zspec(3),
                  pl.BlockSpec((hp, 4, dk), lambda b, h, t: (h, 0, 0)),
                  pl.BlockSpec((hp, 1, dk), lambda b, h, t: (h, 0, 0)),
                  pl.BlockSpec(mc.shape, lambda b, h, t: (0, 0)),
                  pl.BlockSpec(pm.shape, lambda b, h, t: (0, 0, 0))],
        out_specs=pl.BlockSpec((tt, wb), lambda b, h, t: (b * nt + t, h)),
        out_shape=jax.ShapeDtypeStruct((T, d_hg), BF16),
        scratch_shapes=[pltpu.VMEM((hp, dk, dk), F32)],
        compiler_params=pltpu.CompilerParams(
            dimension_semantics=("parallel", "parallel", "arbitrary")),
        name="hgrn2",
    )(z, z, z, z, lbc, gain.reshape(heads, 1, dk), mc, pm)


def _lb_kernel(p_ref, o_ref):
    depth = p_ref.shape[0]
    p = p_ref[...]
    mx = jnp.max(p, axis=0, keepdims=True)
    ex = jnp.exp(p - mx)
    sm = ex / jnp.sum(ex, axis=0, keepdims=True)
    c0 = sm[0:1, :]
    c = c0
    for l in range(depth):
        if l > 0:
            c = c + sm[l:l + 1, :]
        lb = c - c0
        o_ref[l, 0:1, :] = jnp.log(lb)
        o_ref[l, 1:2, :] = jnp.log1p(-lb)
        o_ref[l, 2:3, :] = 1.0 - lb
        o_ref[l, 3:4, :] = lb


def _lb_constants(lb_param):
    depth, d_hg = lb_param.shape
    return pl.pallas_call(
        _lb_kernel,
        out_shape=jax.ShapeDtypeStruct((depth, 4, d_hg), F32),
        name="hgrn2_lower_bounds",
    )(lb_param)


def kernel(x, ffn1_norm, ffn1_w_gate, ffn1_w_up, ffn1_w_down, mix_norm, w_in, conv_w, conv_b, rg_wa, rg_ba, rg_wx, rg_bx, rg_lambda, hg_lower_bounds, hg_out_norm, w_out_a, w_out_b, w_out, ffn2_norm, ffn2_w_gate, ffn2_w_up, ffn2_w_down, final_norm):
    batch, seq, d_model = x.shape
    depth = w_in.shape[0]
    d_rg = rg_lambda.shape[1]
    d_hg = hg_lower_bounds.shape[1]
    heads_hg = d_hg // HG_HEAD_DIM
    T = batch * seq
    tm, tn = ROW_TILE, COL_TILE

    lbc = _lb_constants(hg_lower_bounds)
    lbc = lbc.reshape(depth, 4, heads_hg, HG_HEAD_DIM).transpose(0, 2, 1, 3)

    bf = lambda w: w.astype(BF16)
    names = ("wg1", "wu1", "wd1", "w_in", "wa", "wb", "w_out", "wg2", "wu2", "wd2")
    stacks = dict(zip(names, (ffn1_w_gate, ffn1_w_up, ffn1_w_down, w_in, w_out_a, w_out_b, w_out,
                              ffn2_w_gate, ffn2_w_up, ffn2_w_down)))
    wt = {(k, 0): bf(stacks[k][0]) for k in ("wg1", "wu1", "wd1")}
    rg_wa16, rg_wx16 = bf(rg_wa), bf(rg_wx)

    def run(fn, to_cast, *args, **kw):
        todo = [(k, ll) for k, ll in to_cast if ll < depth and (k, ll) not in wt]
        out, cast = fn(*args, [(stacks[k], ll) for k, ll in todo], **kw)
        wt.update(zip(todo, cast))
        return out

    h = x.reshape(T, d_model)
    for l in range(depth):
        early = [("w_in", l), ("wa", l), ("wb", l)]
        act = run(_norm_mm, [("wg1", l + 1), ("wu1", l + 1)] + early,
                  h, ffn1_norm[l], [wt["wg1", l], wt["wu1", l]], tm=tm // 2, tn=tn)
        early = [("w_out", l), ("wg2", l), ("wu2", l), ("wd2", l)]
        h = run(_mm_res, [("wd1", l + 1)] + early,
                act, wt["wd1", l], h, FFN_RES_WEIGHT, tm=tm, tn=tn, in_place=l > 0)

        z = run(_norm_mm, [("w_in", l + 1)], h, mix_norm[l], [wt["w_in", l]], tm=tm // 2, tn=2 * tn)
        y_a = _rglru(z, conv_w[l], conv_b[l], rg_wa16[l], rg_ba[l], rg_wx16[l], rg_bx[l], rg_lambda[l],
                     batch=batch, seq=seq, d_rg=d_rg, tt=SEQ_TILE)
        y_b = _hgrn2(z, lbc[l], hg_out_norm[l], batch=batch, seq=seq, d_hg=d_hg,
                     col0=2 * d_rg, tt=HG_SEQ_TILE, sub=HG_SUB, chunk=HG_CHUNK)
        ga_col = 2 * d_rg + 4 * d_hg
        merged = run(_merge, [("wa", l + 1), ("wb", l + 1)], y_a, y_b, wt["wa", l], wt["wb", l], z,
                     ga_col, ga_col + d_model, tm=tm, tn=tn)
        h = run(_mm_res, [("w_out", l + 1)], merged, wt["w_out", l], h, 1.0, tm=tm, tn=tn, in_place=True)

        act = run(_norm_mm, [("wg2", l + 1), ("wu2", l + 1)],
                  h, ffn2_norm[l], [wt["wg2", l], wt["wu2", l]], tm=tm // 2, tn=tn)
        h = run(_mm_res, [("wd2", l + 1)], act, wt["wd2", l], h, FFN_RES_WEIGHT, tm=tm, tn=tn, in_place=True)
    out = _rmsnorm(h, final_norm, tr=256)
    return out.reshape(batch, seq, d_model)
```
